```python
import math
import jax, jax.numpy as jnp
from jax import lax
import numpy as np

D_MODEL = 2048
BATCH = 2
SEQ = 4096
DEPTH = 1
DEC_BATCH = 4
DEC_SEQ = 4096
PAST_LEN = 128

ATT_HEADS = 8
ATT_QK_DIM = 64
ATT_V_DIM = 2 * ATT_QK_DIM
ATT_WIDTH = ATT_HEADS * ATT_V_DIM
MLSTM_HEADS = 4
MLSTM_HEAD_DIM = 256
MLSTM_WIDTH = MLSTM_HEADS * MLSTM_HEAD_DIM
MLSTM_CHUNK = 128
CONV_WIDTH = 5
MIX_WIDTH = ATT_WIDTH + MLSTM_WIDTH
ATT_QK_COLS = ATT_HEADS * 2 * ATT_QK_DIM
N_GATES = 4 * MLSTM_HEADS
SPLIT_SIZES = (ATT_QK_COLS, ATT_QK_COLS, ATT_WIDTH,
               MLSTM_WIDTH, MLSTM_WIDTH, MLSTM_WIDTH, MLSTM_WIDTH, N_GATES)
IN_WIDTH = sum(SPLIT_SIZES)
FFN_DIM = 5632
REL_BUCKETS = 32
REL_MAX_DIST = 128
Q_BLOCK = 128
EPS = 1e-6

kernel_name = "hymba_diffattn_mlstm_encoder"


def rmsnorm(x, g):
    xf = x.astype(jnp.float32)
    y = xf * lax.rsqrt(jnp.mean(xf * xf, axis=-1, keepdims=True) + EPS)
    return (y * g.astype(jnp.float32)).astype(x.dtype)


def swiglu(x, w_gate, w_up, w_down):
    return (jax.nn.silu(x @ w_gate) * (x @ w_up)) @ w_down


def rel_bucket(rel):
    nb = REL_BUCKETS // 2
    max_exact = nb // 2
    ret = jnp.where(rel > 0, nb, 0)
    n = jnp.abs(rel)
    nf = jnp.maximum(n, 1).astype(jnp.float32)
    large = max_exact + (jnp.log(nf / max_exact) / math.log(REL_MAX_DIST / max_exact)
                         * (nb - max_exact)).astype(jnp.int32)
    large = jnp.minimum(large, nb - 1)
    return ret + jnp.where(n < max_exact, n, large)


def diff_attention(q, k, v, rel_bias, lam, lambda_init, head_g):
    B, S = q.shape[0], q.shape[1]
    nb = S // Q_BLOCK
    qb = (q * (ATT_QK_DIM ** -0.5)).reshape(B, nb, Q_BLOCK, ATT_HEADS, 2, ATT_QK_DIM)
    qb = jnp.moveaxis(qb, 1, 0)
    key_pos = jnp.arange(S)

    def block(args):
        i, q_i = args
        q_pos = i * Q_BLOCK + jnp.arange(Q_BLOCK)
        bucket = rel_bucket(key_pos[None, :] - q_pos[:, None])
        bias = rel_bias[bucket].astype(jnp.float32)
        bias = bias.reshape(Q_BLOCK, S, ATT_HEADS, 2).transpose(2, 3, 0, 1)
        logits = jnp.einsum('bqhcd,bkhcd->bhcqk', q_i, k).astype(jnp.float32) + bias
        p = jax.nn.softmax(logits, axis=-1)
        w = p[:, :, 0] - lam * p[:, :, 1]
        return jnp.einsum('bhqk,bkhe->bqhe', w.astype(v.dtype), v)

    o = lax.map(block, (jnp.arange(nb), qb))
    o = jnp.moveaxis(o, 0, 1).reshape(B, S, ATT_HEADS, ATT_V_DIM)
    o = rmsnorm(o, head_g.reshape(ATT_HEADS, ATT_V_DIM)) * (1.0 - lambda_init)
    return o.reshape(B, S, ATT_WIDTH)


def mlstm_chunk_scan(q, k, v, ig, lf):
    B, H, S, d = q.shape
    L = MLSTM_CHUNK
    nc = S // L
    f32 = jnp.float32

    def chunks(a):
        return jnp.moveaxis(a.astype(f32).reshape(B, H, nc, L, *a.shape[3:]), 2, 0)

    causal = jnp.tril(jnp.ones((L, L), dtype=bool))

    def step(carry, inp):
        C, n, m = carry
        qc, kc, vc, ic, fc = inp
        b = jnp.cumsum(fc, axis=-1)
        D = jnp.where(causal, b[..., :, None] - b[..., None, :] + ic[..., None, :], -jnp.inf)
        inter = b + m[..., None]
        m_t = jnp.maximum(inter, jnp.max(D, axis=-1))
        Dw = jnp.exp(D - m_t[..., None])
        iw = jnp.exp(inter - m_t)
        s = jnp.einsum('bhtd,bhsd->bhts', qc, kc) * Dw
        num = iw[..., None] * jnp.einsum('bhed,bhtd->bhte', C, qc) + jnp.einsum('bhts,bhse->bhte', s, vc)
        den = iw * jnp.einsum('bhd,bhtd->bht', n, qc) + jnp.sum(s, axis=-1)
        h = num / jnp.maximum(jnp.abs(den), jnp.exp(-m_t))[..., None]
        bL = b[..., -1]
        g = bL[..., None] - b + ic
        m_new = jnp.maximum(bL + m, jnp.max(g, axis=-1))
        decay = jnp.exp(bL + m - m_new)
        wk = jnp.exp(g - m_new[..., None])
        C_new = decay[..., None, None] * C + jnp.einsum('bhs,bhse,bhsd->bhed', wk, vc, kc)
        n_new = decay[..., None] * n + jnp.einsum('bhs,bhsd->bhd', wk, kc)
        return (C_new, n_new, m_new), h

    init = (jnp.zeros((B, H, d, d), f32), jnp.zeros((B, H, d), f32), jnp.zeros((B, H), f32))
    _, hs = lax.scan(step, init, (chunks(q), chunks(k), chunks(v), chunks(ig), chunks(lf)))
    return jnp.moveaxis(hs, 0, 2).reshape(B, H, S, d)


def centred_conv(x, w, b):
    pad = CONV_WIDTH // 2
    S = x.shape[1]
    xp = jnp.pad(x, ((0, 0), (pad, pad), (0, 0)))
    out = b
    for j in range(CONV_WIDTH):
        out = out + xp[:, j:j + S] * w[j]
    return jax.nn.silu(out)


def mix_layer(h, w_in, gate_bias, conv_w, conv_b, rel_bias, lam, lambda_init,
              att_head_g, mlstm_head_g, w_out):
    B, S = h.shape[0], h.shape[1]
    proj = h @ w_in
    idx = list(np.cumsum(SPLIT_SIZES)[:-1])
    a_q, a_k, a_v, m_q, m_k, m_v, m_o, gates = jnp.split(proj, idx, axis=-1)
    a_q = a_q.reshape(B, S, ATT_HEADS, 2, ATT_QK_DIM)
    a_k = a_k.reshape(B, S, ATT_HEADS, 2, ATT_QK_DIM)
    a_v = a_v.reshape(B, S, ATT_HEADS, ATT_V_DIM)
    att_out = diff_attention(a_q, a_k, a_v, rel_bias, lam, lambda_init, att_head_g)
    qk = centred_conv(jnp.concatenate([m_q, m_k], axis=-1), conv_w, conv_b)
    m_q, m_k = jnp.split(qk, 2, axis=-1)

    def heads(a):
        return a.reshape(B, S, MLSTM_HEADS, MLSTM_HEAD_DIM).transpose(0, 2, 1, 3)

    mq = heads(m_q) * (MLSTM_HEAD_DIM ** -0.5)
    mk, mv = heads(m_k), heads(m_v)
    g = (gates.astype(jnp.float32) + gate_bias.astype(jnp.float32)).transpose(0, 2, 1)
    i_f, i_b, f_f, f_b = jnp.split(g, 4, axis=1)
    lf_f, lf_b = jax.nn.log_sigmoid(f_f), jax.nn.log_sigmoid(f_b)
    h_fwd = mlstm_chunk_scan(mq, mk, mv, i_f, lf_f)
    flip = lambda a: jnp.flip(a, axis=2)
    h_bwd = flip(mlstm_chunk_scan(flip(mq), flip(mk), flip(mv), flip(i_b), flip(lf_b)))
    hm = (h_fwd + h_bwd).transpose(0, 2, 1, 3)
    hm = rmsnorm(hm, mlstm_head_g.reshape(MLSTM_HEADS, MLSTM_HEAD_DIM))
    o_gate = jax.nn.sigmoid(m_o.astype(jnp.float32)).reshape(B, S, MLSTM_HEADS, MLSTM_HEAD_DIM)
    mlstm_out = (o_gate * hm).reshape(B, S, MLSTM_WIDTH).astype(h.dtype)
    return jnp.concatenate([att_out.astype(h.dtype), mlstm_out], axis=-1) @ w_out


def setup_inputs(seed: int = 0) -> dict:
    key = jax.random.key(seed)
    ks = jax.random.split(key, 32)
    f32 = jnp.float32
    nrm = lambda k, shape, s: jax.random.normal(k, shape, f32) * s
    gain = lambda k: 1.0 + nrm(k, (DEPTH, D_MODEL), 0.02)
    forget_bias = jnp.tile(jnp.linspace(3.0, 6.0, MLSTM_HEADS, dtype=f32), 2)
    gate_bias = jnp.concatenate([nrm(ks[10], (DEPTH, 2 * MLSTM_HEADS), 0.1),
                                 forget_bias[None, :] + nrm(ks[11], (DEPTH, 2 * MLSTM_HEADS), 0.1)], axis=-1)
    return {
        "x_prompt": nrm(ks[0], (BATCH, SEQ, D_MODEL), 1.0),
        "x_sample": nrm(ks[1], (DEC_BATCH, DEC_SEQ, D_MODEL), 1.0),
        "rel_bias": nrm(ks[2], (REL_BUCKETS, 2 * ATT_HEADS), 0.5),
        "ffn1_pre_g": gain(ks[3]),
        "ffn1_post_g": gain(ks[4]),
        "ffn1_w_gate": nrm(ks[5], (DEPTH, D_MODEL, FFN_DIM), D_MODEL ** -0.5),
        "ffn1_w_up": nrm(ks[6], (DEPTH, D_MODEL, FFN_DIM), D_MODEL ** -0.5),
        "ffn1_w_down": nrm(ks[7], (DEPTH, FFN_DIM, D_MODEL), FFN_DIM ** -0.5),
        "mix_pre_g": gain(ks[8]),
        "mix_post_g": gain(ks[9]),
        "w_in": nrm(ks[12], (DEPTH, D_MODEL, IN_WIDTH), D_MODEL ** -0.5),
        "gate_bias": gate_bias,
        "conv_w": nrm(ks[13], (DEPTH, CONV_WIDTH, 2 * MLSTM_WIDTH), CONV_WIDTH ** -0.5),
        "conv_b": nrm(ks[14], (DEPTH, 2 * MLSTM_WIDTH), 0.01),
        "lambda_q1": nrm(ks[15], (DEPTH, ATT_QK_DIM), 0.1),
        "lambda_k1": nrm(ks[16], (DEPTH, ATT_QK_DIM), 0.1),
        "lambda_q2": nrm(ks[17], (DEPTH, ATT_QK_DIM), 0.1),
        "lambda_k2": nrm(ks[18], (DEPTH, ATT_QK_DIM), 0.1),
        "att_head_g": 1.0 + nrm(ks[19], (DEPTH, ATT_WIDTH), 0.02),
        "mlstm_head_g": 1.0 + nrm(ks[20], (DEPTH, MLSTM_WIDTH), 0.02),
        "w_out": nrm(ks[21], (DEPTH, MIX_WIDTH, D_MODEL), MIX_WIDTH ** -0.5),
        "ffn2_pre_g": gain(ks[22]),
        "ffn2_post_g": gain(ks[23]),
        "ffn2_w_gate": nrm(ks[24], (DEPTH, D_MODEL, FFN_DIM), D_MODEL ** -0.5),
        "ffn2_w_up": nrm(ks[25], (DEPTH, D_MODEL, FFN_DIM), D_MODEL ** -0.5),
        "ffn2_w_down": nrm(ks[26], (DEPTH, FFN_DIM, D_MODEL), FFN_DIM ** -0.5),
    }


def reference(x_prompt, x_sample, rel_bias,
              ffn1_pre_g, ffn1_post_g, ffn1_w_gate, ffn1_w_up, ffn1_w_down,
              mix_pre_g, mix_post_g, w_in, gate_bias, conv_w, conv_b,
              lambda_q1, lambda_k1, lambda_q2, lambda_k2,
              att_head_g, mlstm_head_g, w_out,
              ffn2_pre_g, ffn2_post_g, ffn2_w_gate, ffn2_w_up, ffn2_w_down):
    def trunk(x):
        for l in range(DEPTH):
            lambda_init = 0.8 - 0.6 * math.exp(-0.3 * l)
            lam = (jnp.exp(jnp.sum(lambda_q1[l].astype(jnp.float32) * lambda_k1[l].astype(jnp.float32)))
                   - jnp.exp(jnp.sum(lambda_q2[l].astype(jnp.float32) * lambda_k2[l].astype(jnp.float32)))
                   + lambda_init)
            h = rmsnorm(x, ffn1_pre_g[l])
            x = x + 0.5 * rmsnorm(swiglu(h, ffn1_w_gate[l], ffn1_w_up[l], ffn1_w_down[l]), ffn1_post_g[l])
            h = rmsnorm(x, mix_pre_g[l])
            mixed = mix_layer(h, w_in[l], gate_bias[l], conv_w[l], conv_b[l], rel_bias, lam,
                              lambda_init, att_head_g[l], mlstm_head_g[l], w_out[l])
            x = x + rmsnorm(mixed, mix_post_g[l])
            h = rmsnorm(x, ffn2_pre_g[l])
            x = x + 0.5 * rmsnorm(swiglu(h, ffn2_w_gate[l], ffn2_w_up[l], ffn2_w_down[l]), ffn2_post_g[l])
        return x

    y_prompt = trunk(x_prompt)
    y_sample = trunk(x_sample)
    return (y_prompt, y_sample)
```

```python
import functools
import math

import jax
import jax.numpy as jnp
from jax import lax
from jax.experimental import pallas as pl
from jax.experimental.pallas import tpu as pltpu

F32 = jnp.float32
BF16 = jnp.bfloat16

EPS = 1e-6
ATT_HEADS = 8
ATT_QK_DIM = 64
ATT_V_DIM = 128
ATT_WIDTH = ATT_HEADS * ATT_V_DIM
MLSTM_HEADS = 4
MLSTM_HEAD_DIM = 256
MLSTM_WIDTH = MLSTM_HEADS * MLSTM_HEAD_DIM
MLSTM_CHUNK = 128
CONV_WIDTH = 5
N_GATES = 4 * MLSTM_HEADS
MAIN_COLS = 3 * ATT_WIDTH + 4 * MLSTM_WIDTH
REL_BUCKETS = 32
REL_MAX_DIST = 128
LAMBDA_INIT = 0.8 - 0.6 * math.exp(-0.3 * 0)

LANES = 128
VMEM_LIMIT = 56 * 1024 * 1024


def _params(n_axes):
    return pltpu.CompilerParams(dimension_semantics=("arbitrary",) * n_axes,
                                vmem_limit_bytes=VMEM_LIMIT)


def _rms(xf, g_row):
    ms = jnp.mean(xf * xf, axis=-1, keepdims=True)
    return xf * lax.rsqrt(ms + EPS) * g_row


def _ffn_body(x_ref, pre_g_ref, post_g_ref, wg_ref, wu_ref, wd_ref, o_ref, h_ref, acc_ref):
    f = pl.program_id(1)

    @pl.when(f == 0)
    def _():
        h_ref[...] = _rms(x_ref[...], pre_g_ref[...]).astype(BF16)

    h = h_ref[...]
    g = jnp.dot(h, wg_ref[...], preferred_element_type=F32)
    u = jnp.dot(h, wu_ref[...], preferred_element_type=F32)
    a = (g * jax.nn.sigmoid(g) * u).astype(BF16)
    part = jnp.dot(a, wd_ref[...], preferred_element_type=F32)

    @pl.when(f == 0)
    def _():
        acc_ref[...] = part

    @pl.when(f > 0)
    def _():
        acc_ref[...] += part

    @pl.when(f == pl.num_programs(1) - 1)
    def _():
        o_ref[...] = x_ref[...] + 0.5 * _rms(acc_ref[...], post_g_ref[...])


def _ffn(x, pre_g, post_g, wg, wu, wd, *, tm, tf):
    t, d = x.shape
    fdim = wg.shape[1]
    return pl.pallas_call(
        _ffn_body,
        out_shape=jax.ShapeDtypeStruct((t, d), F32),
        grid=(t // tm, fdim // tf),
        in_specs=[
            pl.BlockSpec((tm, d), lambda i, f: (i, 0)),
            pl.BlockSpec((1, d), lambda i, f: (0, 0)),
            pl.BlockSpec((1, d), lambda i, f: (0, 0)),
            pl.BlockSpec((d, tf), lambda i, f: (0, f)),
            pl.BlockSpec((d, tf), lambda i, f: (0, f)),
            pl.BlockSpec((tf, d), lambda i, f: (f, 0)),
        ],
        out_specs=pl.BlockSpec((tm, d), lambda i, f: (i, 0)),
        scratch_shapes=[pltpu.VMEM((tm, d), BF16), pltpu.VMEM((tm, d), F32)],
        compiler_params=_params(2),
        name="ffn",
    )(x, pre_g, post_g, wg, wu, wd)


def _inproj_body(x_ref, g_ref, w_ref, wgate_ref, gbias_ref, o_ref, gates_ref, h_ref):
    j = pl.program_id(1)

    @pl.when(j == 0)
    def _():
        h = _rms(x_ref[...], g_ref[...]).astype(BF16)
        h_ref[...] = h
        gates_ref[...] = jnp.dot(h, wgate_ref[...], preferred_element_type=F32) + gbias_ref[...]

    o_ref[...] = jnp.dot(h_ref[...], w_ref[...], preferred_element_type=F32).astype(BF16)


def _in_proj(x, g, w_main, w_gate, gate_bias, *, tm, tn):
    t, d = x.shape
    n = w_main.shape[1]
    return pl.pallas_call(
        _inproj_body,
        out_shape=(jax.ShapeDtypeStruct((t, n), BF16), jax.ShapeDtypeStruct((t, LANES), F32)),
        grid=(t // tm, n // tn),
        in_specs=[
            pl.BlockSpec((tm, d), lambda i, j: (i, 0)),
            pl.BlockSpec((1, d), lambda i, j: (0, 0)),
            pl.BlockSpec((d, tn), lambda i, j: (0, j)),
            pl.BlockSpec((d, LANES), lambda i, j: (0, 0)),
            pl.BlockSpec((1, LANES), lambda i, j: (0, 0)),
        ],
        out_specs=(pl.BlockSpec((tm, tn), lambda i, j: (i, j)),
                   pl.BlockSpec((tm, LANES), lambda i, j: (i, 0))),
        scratch_shapes=[pltpu.VMEM((tm, d), BF16)],
        compiler_params=_params(2),
        name="in_proj",
    )(x, g, w_main, w_gate, gate_bias)


def _rel_bucket(rel):
    nb = REL_BUCKETS // 2
    max_exact = nb // 2
    ret = jnp.where(rel > 0, nb, 0)
    n = jnp.abs(rel)
    nf = jnp.maximum(n, 1).astype(jnp.float32)
    large = max_exact + (jnp.log(nf / max_exact) / math.log(REL_MAX_DIST / max_exact)
                         * (nb - max_exact)).astype(jnp.int32)
    large = jnp.minimum(large, nb - 1)
    return ret + jnp.where(n < max_exact, n, large)


def _bias_tiles_body(relb_ref, bucket_ref, o_ref):
    hc = pl.program_id(0)
    bucket = bucket_ref[...]
    acc = jnp.zeros(bucket.shape, F32)
    for b in range(REL_BUCKETS):
        acc = jnp.where(bucket == b, relb_ref[b, hc], acc)
    o_ref[...] = acc


def _bias_tiles(rel_bias):
    row = lax.broadcasted_iota(jnp.int32, (3, LANES, LANES), 1)
    col = lax.broadcasted_iota(jnp.int32, (3, LANES, LANES), 2)
    off = (lax.broadcasted_iota(jnp.int32, (3, LANES, LANES), 0) - 1) * LANES
    bucket = _rel_bucket(off + col - row).astype(jnp.int32)
    n_hc = rel_bias.shape[1]
    return pl.pallas_call(
        _bias_tiles_body,
        out_shape=jax.ShapeDtypeStruct((n_hc, 3, LANES, LANES), F32),
        grid=(n_hc,),
        in_specs=[
            pl.BlockSpec(memory_space=pltpu.SMEM),
            pl.BlockSpec((3, LANES, LANES), lambda i: (0, 0, 0)),
        ],
        out_specs=pl.BlockSpec((None, 3, LANES, LANES), lambda i: (i, 0, 0, 0)),
        compiler_params=_params(1),
        name="rel_bias_tiles",
    )(rel_bias, bucket)


def _attn_body(relb_ref, lam_ref, q_ref, k_ref, v_ref, btile_ref, hg_ref, o_ref,
               kz_ref, vaug_ref, bnear_ref, acc_ref, m_ref, *, tile):
    h = pl.program_id(1)
    i = pl.program_id(2)
    seq = k_ref.shape[0]
    n_tiles = seq // tile
    n_sub = tile // LANES
    far_bucket_neg = REL_BUCKETS // 2 - 1
    far_bucket_pos = REL_BUCKETS - 1

    @pl.when(i == 0)
    def _():
        k = k_ref[...]
        lane = lax.broadcasted_iota(jnp.int32, k.shape, 1)
        zero = jnp.zeros_like(k)
        kz_ref[0] = jnp.where(lane < ATT_QK_DIM, k, zero)
        kz_ref[1] = jnp.where(lane >= ATT_QK_DIM, k, zero)
        vaug_ref[:, :ATT_V_DIM] = v_ref[...]
        vlane = lax.broadcasted_iota(jnp.int32, (seq, LANES), 1)
        vaug_ref[:, ATT_V_DIM:] = jnp.where(vlane == 0, 1.0, 0.0).astype(BF16)
        for c in range(2):
            cneg = relb_ref[far_bucket_neg, 2 * h + c]
            cpos = relb_ref[far_bucket_pos, 2 * h + c]
            for di in range(3):
                for qi in range(n_sub):
                    for kj in range(n_sub):
                        d = (di - 1) * n_sub + kj - qi
                        if -1 <= d <= 1:
                            sub = btile_ref[c, d + 1]
                        else:
                            sub = jnp.full((LANES, LANES), cneg if d < 0 else cpos, F32)
                        bnear_ref[c, di, qi * LANES:(qi + 1) * LANES, kj * LANES:(kj + 1) * LANES] = sub

    m_ref[...] = jnp.full(m_ref.shape, -jnp.inf, F32)
    acc_ref[...] = jnp.zeros(acc_ref.shape, F32)
    q = q_ref[...] * (ATT_QK_DIM ** -0.5)

    def tile_step(j, near):
        start = pl.multiple_of(j * tile, tile)
        vt = vaug_ref[pl.ds(start, tile), :]
        for c in range(2):
            kt = kz_ref[c, pl.ds(start, tile), :]
            s = lax.dot_general(q, kt, (((1,), (1,)), ((), ())), preferred_element_type=F32)
            if near:
                s = s + bnear_ref[c, j - i + 1]
                shift = 0.0
            else:
                shift = jnp.where(j < i, relb_ref[far_bucket_neg, 2 * h + c],
                                  relb_ref[far_bucket_pos, 2 * h + c])
            m_old = m_ref[c]
            m_new = jnp.maximum(m_old, jnp.max(s, axis=-1, keepdims=True) + shift)
            alpha = jnp.exp(m_old - m_new)
            e = jnp.exp(s - (m_new - shift)).astype(BF16)
            acc_ref[c] = alpha * acc_ref[c] + jnp.dot(e, vt, preferred_element_type=F32)
            m_ref[c] = m_new

    def body(j, carry):
        near = jnp.abs(j - i) <= 1

        @pl.when(near)
        def _():
            tile_step(j, True)

        @pl.when(jnp.logical_not(near))
        def _():
            tile_step(j, False)

        return carry

    lax.fori_loop(0, n_tiles, body, 0)

    lam_v = lam_ref[...]
    s1 = jnp.sum(lam_v[0:1] * lam_v[1:2], axis=-1, keepdims=True)
    s2 = jnp.sum(lam_v[2:3] * lam_v[3:4], axis=-1, keepdims=True)
    lam = jnp.exp(s1) - jnp.exp(s2) + LAMBDA_INIT
    outs = []
    for c in range(2):
        acc = acc_ref[c]
        outs.append(acc[:, :ATT_V_DIM] / acc[:, ATT_V_DIM:ATT_V_DIM + 1])
    o = outs[0] - lam * outs[1]
    o_ref[...] = (_rms(o, hg_ref[...]) * (1.0 - LAMBDA_INIT)).astype(BF16)


def _attention(proj, btiles, rel_bias, lam_vecs, head_g, *, tile):
    nb, seq, _ = proj.shape
    assert seq % tile == 0 and tile % LANES == 0
    btiles = btiles.reshape(ATT_HEADS, 2, 3, LANES, LANES)
    return pl.pallas_call(
        functools.partial(_attn_body, tile=tile),
        out_shape=jax.ShapeDtypeStruct((nb, seq, ATT_WIDTH), BF16),
        grid=(nb, ATT_HEADS, seq // tile),
        in_specs=[
            pl.BlockSpec(memory_space=pltpu.SMEM),
            pl.BlockSpec((4, ATT_QK_DIM), lambda b, h, i: (0, 0)),
            pl.BlockSpec((None, tile, LANES), lambda b, h, i: (b, i, h)),
            pl.BlockSpec((None, seq, LANES), lambda b, h, i: (b, 0, ATT_HEADS + h)),
            pl.BlockSpec((None, seq, LANES), lambda b, h, i: (b, 0, 2 * ATT_HEADS + h)),
            pl.BlockSpec((None, 2, 3, LANES, LANES), lambda b, h, i: (h, 0, 0, 0, 0)),
            pl.BlockSpec((1, ATT_V_DIM), lambda b, h, i: (0, h)),
        ],
        out_specs=pl.BlockSpec((None, tile, ATT_V_DIM), lambda b, h, i: (b, i, h)),
        scratch_shapes=[
            pltpu.VMEM((2, seq, LANES), BF16),
            pltpu.VMEM((seq, 2 * ATT_V_DIM), BF16),
            pltpu.VMEM((2, 3, tile, tile), F32),
            pltpu.VMEM((2, tile, 2 * ATT_V_DIM), F32),
            pltpu.VMEM((2, tile, 1), F32),
        ],
        compiler_params=_params(3),
        name="diff_attention",
    )(rel_bias, lam_vecs, proj, proj, proj, btiles, head_g)


def _log_sigmoid(x):
    return jnp.minimum(x, 0.0) - jnp.log1p(jnp.exp(-jnp.abs(x)))


def _lane_scan(x, reverse):
    lane = lax.broadcasted_iota(jnp.int32, x.shape, 1)
    sh = 1
    while sh < LANES:
        if reverse:
            x = x + jnp.where(lane < LANES - sh, pltpu.roll(x, LANES - sh, 1), 0.0)
        else:
            x = x + jnp.where(lane >= sh, pltpu.roll(x, sh, 1), 0.0)
        sh *= 2
    return x


def _mlstm_body(mq_ref, mk_ref, mv_ref, mo_ref, cwq_ref, cwk_ref, cbq_ref, cbk_ref, gates_ref, hg_ref,
                o_ref, q_s, kt_s, gsc, hbuf, c_s):
    seq, dh = mq_ref.shape
    L = MLSTM_CHUNK
    nc = seq // L
    halo = 16

    def conv_chunk(src_ref, w_ref, b_ref, c):
        start = pl.multiple_of(c * L, L)
        cur = src_ref[pl.ds(start, L), :].astype(F32)
        prev_start = pl.multiple_of(jnp.maximum(start - halo, 0), halo)
        next_start = pl.multiple_of(jnp.minimum(start + L, seq - halo), halo)
        prev = src_ref[pl.ds(prev_start, halo), :].astype(F32) * jnp.where(c > 0, 1.0, 0.0)
        nxt = src_ref[pl.ds(next_start, halo), :].astype(F32) * jnp.where(c < nc - 1, 1.0, 0.0)
        ext = jnp.concatenate([prev, cur, nxt], axis=0)
        w = w_ref[...]
        out = jnp.broadcast_to(b_ref[...], (L, dh))
        pad = CONV_WIDTH // 2
        for t in range(CONV_WIDTH):
            lo = halo + t - pad
            out = out + ext[lo:lo + L, :] * w[t:t + 1, :]
        return out * jax.nn.sigmoid(out)

    def prep(c, carry):
        start = pl.multiple_of(c * L, L)
        qc = conv_chunk(mq_ref, cwq_ref, cbq_ref, c) * (MLSTM_HEAD_DIM ** -0.5)
        q_s[pl.ds(start, L), :] = qc.astype(BF16)
        kc = conv_chunk(mk_ref, cwk_ref, cbk_ref, c)
        kt_s[c] = kc.T.astype(BF16)
        return carry

    lax.fori_loop(0, nc, prep, 0)

    g = gates_ref[...]
    b_f = _lane_scan(_log_sigmoid(g[2]), reverse=False)
    b_b = _lane_scan(_log_sigmoid(g[3]), reverse=True)
    gsc[0] = b_f
    gsc[1] = g[0] - b_f
    gsc[2] = b_b
    gsc[3] = g[1] - b_b

    c_s[...] = jnp.zeros(c_s.shape, F32)

    row_i = lax.broadcasted_iota(jnp.int32, (L, L), 0)
    col_i = lax.broadcasted_iota(jnp.int32, (L, L), 1)
    lane_row = lax.broadcasted_iota(jnp.int32, (1, L), 1)
    ones_col = jnp.where(lax.broadcasted_iota(jnp.int32, (L, LANES), 1) == 0, 1.0, 0.0).astype(BF16)

    def chunk_step(c, m, direction):
        start = pl.multiple_of(c * L, L)
        causal = (col_i <= row_i) if direction == 0 else (col_i >= row_i)
        last_lane = L - 1 if direction == 0 else 0
        q = q_s[pl.ds(start, L), :]
        kt = kt_s[c]
        vaug = jnp.concatenate([mv_ref[pl.ds(start, L), :], ones_col], axis=1)
        brow = gsc[2 * direction, pl.ds(c, 1), :]
        rrow = gsc[2 * direction + 1, pl.ds(c, 1), :]
        bcol = jnp.sum(jnp.where(row_i == col_i, brow, 0.0), axis=1, keepdims=True)
        dmat = jnp.where(causal, bcol + rrow, -jnp.inf)
        dmax = jnp.max(dmat, axis=1, keepdims=True)
        qk = jnp.dot(q, kt, preferred_element_type=F32)
        st = (qk * jnp.exp(dmat - dmax)).astype(BF16)
        p_intra = jnp.dot(st, vaug, preferred_element_type=F32)
        c_old = c_s[direction]
        p_inter = jnp.dot(q, c_old.astype(BF16), preferred_element_type=F32)
        inter = bcol + m
        m_t = jnp.maximum(inter, dmax)
        num_aug = jnp.exp(inter - m_t) * p_inter + jnp.exp(dmax - m_t) * p_intra
        den = num_aug[:, dh:dh + 1]
        hout = num_aug[:, :dh] / jnp.maximum(jnp.abs(den), jnp.exp(-m_t))
        b_last = jnp.sum(jnp.where(lane_row == last_lane, brow, 0.0), axis=1, keepdims=True)
        grow = b_last + rrow
        gmax = jnp.max(grow, axis=1, keepdims=True)
        m_new = jnp.maximum(b_last + m, gmax)
        wk = jnp.exp(grow - gmax)
        ktw = (kt.astype(F32) * wk).astype(BF16)
        upd = jnp.dot(ktw, vaug, preferred_element_type=F32)
        c_s[direction] = jnp.exp(b_last + m - m_new) * c_old + jnp.exp(gmax - m_new) * upd
        return hout, m_new

    def finalize(c, hm):
        start = pl.multiple_of(c * L, L)
        y = _rms(hm, hg_ref[...])
        og = jax.nn.sigmoid(mo_ref[pl.ds(start, L), :].astype(F32))
        o_ref[pl.ds(start, L), :] = (og * y).astype(BF16)

    def first_half(k, carry):
        m_f, m_b = carry
        c_f, c_b = k, nc - 1 - k
        h_f, m_f = chunk_step(c_f, m_f, 0)
        h_b, m_b = chunk_step(c_b, m_b, 1)
        hbuf[pl.ds(pl.multiple_of(c_f * L, L), L), :] = h_f
        hbuf[pl.ds(pl.multiple_of(c_b * L, L), L), :] = h_b
        return m_f, m_b

    def second_half(k, carry):
        m_f, m_b = carry
        c_f, c_b = k, nc - 1 - k
        h_f, m_f = chunk_step(c_f, m_f, 0)
        h_b, m_b = chunk_step(c_b, m_b, 1)
        finalize(c_f, h_f + hbuf[pl.ds(pl.multiple_of(c_f * L, L), L), :])
        finalize(c_b, h_b + hbuf[pl.ds(pl.multiple_of(c_b * L, L), L), :])
        return m_f, m_b

    m0 = jnp.zeros((1, 1), F32)
    carry = lax.fori_loop(0, nc // 2, first_half, (m0, m0))
    lax.fori_loop(nc // 2, nc, second_half, carry)


def _mlstm(proj, gates, conv_w, conv_b, head_g):
    nb, seq, _ = proj.shape
    dh = MLSTM_HEAD_DIM
    nc = seq // MLSTM_CHUNK
    assert seq % MLSTM_CHUNK == 0 and nc % 2 == 0
    base = 3 * ATT_WIDTH // dh
    nh = MLSTM_HEADS

    def col(group):
        return pl.BlockSpec((None, seq, dh), lambda b, h: (b, 0, base + group * nh + h))

    return pl.pallas_call(
        _mlstm_body,
        out_shape=jax.ShapeDtypeStruct((nb, seq, MLSTM_WIDTH), BF16),
        grid=(nb, nh),
        in_specs=[
            col(0), col(1), col(2), col(3),
            pl.BlockSpec((CONV_WIDTH, dh), lambda b, h: (0, h)),
            pl.BlockSpec((CONV_WIDTH, dh), lambda b, h: (0, nh + h)),
            pl.BlockSpec((1, dh), lambda b, h: (0, h)),
            pl.BlockSpec((1, dh), lambda b, h: (0, nh + h)),
            pl.BlockSpec((None, None, 4, nc, MLSTM_CHUNK), lambda b, h: (b, h, 0, 0, 0)),
            pl.BlockSpec((1, dh), lambda b, h: (0, h)),
        ],
        out_specs=pl.BlockSpec((None, seq, dh), lambda b, h: (b, 0, h)),
        scratch_shapes=[
            pltpu.VMEM((seq, dh), BF16),
            pltpu.VMEM((nc, dh, MLSTM_CHUNK), BF16),
            pltpu.VMEM((4, nc, MLSTM_CHUNK), F32),
            pltpu.VMEM((seq, dh), F32),
            pltpu.VMEM((2, dh, dh + LANES), F32),
        ],
        compiler_params=_params(2),
        name="mlstm",
    )(proj, proj, proj, proj, conv_w, conv_w, conv_b, conv_b, gates, head_g)


def _outproj_body(att_ref, ml_ref, wa_ref, wm_ref, x_ref, g_ref, o_ref):
    mixed = (jnp.dot(att_ref[...], wa_ref[...], preferred_element_type=F32)
             + jnp.dot(ml_ref[...], wm_ref[...], preferred_element_type=F32))
    o_ref[...] = x_ref[...] + _rms(mixed, g_ref[...])


def _out_proj(att, ml, w_out, x, g, *, tm):
    t, d = x.shape
    wa = att.shape[1]
    wm = ml.shape[1]
    assert wa == wm
    return pl.pallas_call(
        _outproj_body,
        out_shape=jax.ShapeDtypeStruct((t, d), F32),
        grid=(t // tm,),
        in_specs=[
            pl.BlockSpec((tm, wa), lambda i: (i, 0)),
            pl.BlockSpec((tm, wm), lambda i: (i, 0)),
            pl.BlockSpec((wa, d), lambda i: (0, 0)),
            pl.BlockSpec((wm, d), lambda i: (1, 0)),
            pl.BlockSpec((tm, d), lambda i: (i, 0)),
            pl.BlockSpec((1, d), lambda i: (0, 0)),
        ],
        out_specs=pl.BlockSpec((tm, d), lambda i: (i, 0)),
        compiler_params=_params(1),
        name="out_proj",
    )(att, ml, w_out, w_out, x, g)


def _layer(x, rel_bias, ffn1_pre_g, ffn1_post_g, ffn1_w_gate, ffn1_w_up, ffn1_w_down,
           mix_pre_g, mix_post_g, w_in, gate_bias, conv_w, conv_b, lam_vecs,
           att_head_g, mlstm_head_g, w_out,
           ffn2_pre_g, ffn2_post_g, ffn2_w_gate, ffn2_w_up, ffn2_w_down,
           *, tm, tf, tn, attn_tile):
    nb, seq, d = x.shape
    t = nb * seq
    row = lambda v: v.reshape(1, -1).astype(F32)
    xt = x.reshape(t, d)

    x1 = _ffn(xt, row(ffn1_pre_g), row(ffn1_post_g), ffn1_w_gate.astype(BF16),
              ffn1_w_up.astype(BF16), ffn1_w_down.astype(BF16), tm=tm, tf=tf)

    w_main = w_in[:, :MAIN_COLS].astype(BF16)
    w_gate = jnp.pad(w_in[:, MAIN_COLS:], ((0, 0), (0, LANES - N_GATES))).astype(BF16)
    gbias = jnp.pad(row(gate_bias), ((0, 0), (0, LANES - N_GATES)))
    proj, gates = _in_proj(x1, row(mix_pre_g), w_main, w_gate, gbias, tm=tm, tn=tn)
    proj = proj.reshape(nb, seq, MAIN_COLS)
    nc = seq // MLSTM_CHUNK
    gates = gates[:, :N_GATES].reshape(nb, nc, MLSTM_CHUNK, 4, MLSTM_HEADS).transpose(0, 4, 3, 1, 2)

    btiles = _bias_tiles(rel_bias.astype(F32))
    att = _attention(proj, btiles, rel_bias.astype(F32), lam_vecs, row(att_head_g), tile=attn_tile)
    ml = _mlstm(proj, gates, conv_w.astype(F32), row(conv_b), row(mlstm_head_g))

    x2 = _out_proj(att.reshape(t, ATT_WIDTH), ml.reshape(t, MLSTM_WIDTH), w_out.astype(BF16),
                   x1, row(mix_post_g), tm=tm)
    x3 = _ffn(x2, row(ffn2_pre_g), row(ffn2_post_g), ffn2_w_gate.astype(BF16),
              ffn2_w_up.astype(BF16), ffn2_w_down.astype(BF16), tm=tm, tf=tf)
    return x3.reshape(nb, seq, d)


def kernel(x_prompt, x_sample, rel_bias, ffn1_pre_g, ffn1_post_g, ffn1_w_gate, ffn1_w_up, ffn1_w_down, mix_pre_g, mix_post_g, w_in, gate_bias, conv_w, conv_b, lambda_q1, lambda_k1, lambda_q2, lambda_k2, att_head_g, mlstm_head_g, w_out, ffn2_pre_g, ffn2_post_g, ffn2_w_gate, ffn2_w_up, ffn2_w_down, *, tm=512, tf=512, tn=512, attn_tile=512):
    assert x_prompt.shape[1:] == x_sample.shape[1:]
    assert ffn1_pre_g.shape[0] == 1, "single-layer trunk"
    nbp = x_prompt.shape[0]
    x = jnp.concatenate([x_prompt, x_sample], axis=0)
    lam_vecs = jnp.concatenate([lambda_q1, lambda_k1, lambda_q2, lambda_k2], axis=0).astype(F32)
    y = _layer(x, rel_bias, ffn1_pre_g[0], ffn1_post_g[0], ffn1_w_gate[0], ffn1_w_up[0], ffn1_w_down[0],
               mix_pre_g[0], mix_post_g[0], w_in[0], gate_bias[0], conv_w[0], conv_b[0], lam_vecs,
               att_head_g[0], mlstm_head_g[0], w_out[0],
               ffn2_pre_g[0], ffn2_post_g[0], ffn2_w_gate[0], ffn2_w_up[0], ffn2_w_down[0],
               tm=tm, tf=tf, tn=tn, attn_tile=attn_tile)
    return (y[:nbp], y[nbp:])
```

```python
import functools
import math

import jax
import jax.numpy as jnp
from jax import lax
from jax.experimental import pallas as pl
from jax.experimental.pallas import tpu as pltpu

F32 = jnp.float32
BF16 = jnp.bfloat16

EPS = 1e-6
ATT_HEADS = 8
ATT_QK_DIM = 64
ATT_V_DIM = 128
ATT_WIDTH = ATT_HEADS * ATT_V_DIM
MLSTM_HEADS = 4
MLSTM_HEAD_DIM = 256
MLSTM_WIDTH = MLSTM_HEADS * MLSTM_HEAD_DIM
MLSTM_CHUNK = 128
CONV_WIDTH = 5
N_GATES = 4 * MLSTM_HEADS
MAIN_COLS = 3 * ATT_WIDTH + 4 * MLSTM_WIDTH
REL_BUCKETS = 32
REL_MAX_DIST = 128
LAMBDA_INIT = 0.8 - 0.6 * math.exp(-0.3 * 0)

LANES = 128
VMEM_LIMIT = 56 * 1024 * 1024


def _params(n_axes):
    return pltpu.CompilerParams(dimension_semantics=("arbitrary",) * n_axes,
                                vmem_limit_bytes=VMEM_LIMIT)


def _rms(xf, g_row):
    ms = jnp.mean(xf * xf, axis=-1, keepdims=True)
    return xf * lax.rsqrt(ms + EPS) * g_row


def _ffn_body(x_ref, pre_g_ref, post_g_ref, wg_ref, wu_ref, wd_ref, o_ref, h_ref, acc_ref):
    f = pl.program_id(1)

    @pl.when(f == 0)
    def _():
        h_ref[...] = _rms(x_ref[...], pre_g_ref[...]).astype(BF16)

    h = h_ref[...]
    g = jnp.dot(h, wg_ref[...], preferred_element_type=F32)
    u = jnp.dot(h, wu_ref[...], preferred_element_type=F32)
    a = (g * jax.nn.sigmoid(g) * u).astype(BF16)
    part = jnp.dot(a, wd_ref[...], preferred_element_type=F32)

    @pl.when(f == 0)
    def _():
        acc_ref[...] = part

    @pl.when(f > 0)
    def _():
        acc_ref[...] += part

    @pl.when(f == pl.num_programs(1) - 1)
    def _():
        o_ref[...] = x_ref[...] + 0.5 * _rms(acc_ref[...], post_g_ref[...])


def _ffn(x, pre_g, post_g, wg, wu, wd, *, tm, tf):
    t, d = x.shape
    fdim = wg.shape[1]
    return pl.pallas_call(
        _ffn_body,
        out_shape=jax.ShapeDtypeStruct((t, d), F32),
        grid=(t // tm, fdim // tf),
        in_specs=[
            pl.BlockSpec((tm, d), lambda i, f: (i, 0)),
            pl.BlockSpec((1, d), lambda i, f: (0, 0)),
            pl.BlockSpec((1, d), lambda i, f: (0, 0)),
            pl.BlockSpec((d, tf), lambda i, f: (0, f)),
            pl.BlockSpec((d, tf), lambda i, f: (0, f)),
            pl.BlockSpec((tf, d), lambda i, f: (f, 0)),
        ],
        out_specs=pl.BlockSpec((tm, d), lambda i, f: (i, 0)),
        scratch_shapes=[pltpu.VMEM((tm, d), BF16), pltpu.VMEM((tm, d), F32)],
        compiler_params=_params(2),
        name="ffn",
    )(x, pre_g, post_g, wg, wu, wd)


def _inproj_body(x_ref, g_ref, w_ref, wgate_ref, gbias_ref, o_ref, gates_ref, h_ref):
    j = pl.program_id(1)

    @pl.when(j == 0)
    def _():
        h = _rms(x_ref[...], g_ref[...]).astype(BF16)
        h_ref[...] = h
        gates_ref[...] = jnp.dot(h, wgate_ref[...], preferred_element_type=F32) + gbias_ref[...]

    o_ref[...] = jnp.dot(h_ref[...], w_ref[...], preferred_element_type=F32).astype(BF16)


def _in_proj(x, g, w_main, w_gate, gate_bias, *, tm, tn):
    t, d = x.shape
    n = w_main.shape[1]
    return pl.pallas_call(
        _inproj_body,
        out_shape=(jax.ShapeDtypeStruct((t, n), BF16), jax.ShapeDtypeStruct((t, LANES), F32)),
        grid=(t // tm, n // tn),
        in_specs=[
            pl.BlockSpec((tm, d), lambda i, j: (i, 0)),
            pl.BlockSpec((1, d), lambda i, j: (0, 0)),
            pl.BlockSpec((d, tn), lambda i, j: (0, j)),
            pl.BlockSpec((d, LANES), lambda i, j: (0, 0)),
            pl.BlockSpec((1, LANES), lambda i, j: (0, 0)),
        ],
        out_specs=(pl.BlockSpec((tm, tn), lambda i, j: (i, j)),
                   pl.BlockSpec((tm, LANES), lambda i, j: (i, 0))),
        scratch_shapes=[pltpu.VMEM((tm, d), BF16)],
        compiler_params=_params(2),
        name="in_proj",
    )(x, g, w_main, w_gate, gate_bias)


def _rel_bucket(rel):
    nb = REL_BUCKETS // 2
    max_exact = nb // 2
    ret = jnp.where(rel > 0, nb, 0)
    n = jnp.abs(rel)
    nf = jnp.maximum(n, 1).astype(jnp.float32)
    large = max_exact + (jnp.log(nf / max_exact) / math.log(REL_MAX_DIST / max_exact)
                         * (nb - max_exact)).astype(jnp.int32)
    large = jnp.minimum(large, nb - 1)
    return ret + jnp.where(n < max_exact, n, large)


def _bias_tiles_body(relb_ref, bucket_ref, o_ref):
    hc = pl.program_id(0)
    bucket = bucket_ref[...]
    acc = jnp.zeros(bucket.shape, F32)
    for b in range(REL_BUCKETS):
        acc = jnp.where(bucket == b, relb_ref[b, hc], acc)
    o_ref[...] = acc


def _bias_tiles(rel_bias):
    key = lax.broadcasted_iota(jnp.int32, (3, LANES, LANES), 1)
    query = lax.broadcasted_iota(jnp.int32, (3, LANES, LANES), 2)
    off = (lax.broadcasted_iota(jnp.int32, (3, LANES, LANES), 0) - 1) * LANES
    bucket = _rel_bucket(off + key - query).astype(jnp.int32)
    n_hc = rel_bias.shape[1]
    return pl.pallas_call(
        _bias_tiles_body,
        out_shape=jax.ShapeDtypeStruct((n_hc, 3, LANES, LANES), F32),
        grid=(n_hc,),
        in_specs=[
            pl.BlockSpec(memory_space=pltpu.SMEM),
            pl.BlockSpec((3, LANES, LANES), lambda i: (0, 0, 0)),
        ],
        out_specs=pl.BlockSpec((None, 3, LANES, LANES), lambda i: (i, 0, 0, 0)),
        compiler_params=_params(1),
        name="rel_bias_tiles",
    )(rel_bias, bucket)


V_ROWS = ATT_V_DIM + 16


def _attn_body(relb_ref, lam_ref, q_ref, k_ref, v_ref, btile_ref, hg_ref, o_ref,
               vt_ref, bias_ref, sa_ref, sb_ref, ma_ref, mb_ref, acc_ref, m_ref, *, tile):
    h = pl.program_id(1)
    i = pl.program_id(2)
    seq = k_ref.shape[0]
    n_tiles = seq // tile
    n_sub = tile // LANES
    far_bucket_neg = REL_BUCKETS // 2 - 1
    far_bucket_pos = REL_BUCKETS - 1

    @pl.when(i == 0)
    def _():
        ones_rows = jnp.where(lax.broadcasted_iota(jnp.int32, (V_ROWS - ATT_V_DIM, tile), 0) == 0,
                              1.0, 0.0).astype(BF16)
        for jt in range(n_tiles):
            vt_ref[jt, :ATT_V_DIM, :] = v_ref[jt * tile:(jt + 1) * tile, :].T
            vt_ref[jt, ATT_V_DIM:, :] = ones_rows
        for c in range(2):
            cneg = relb_ref[far_bucket_neg, 2 * h + c]
            cpos = relb_ref[far_bucket_pos, 2 * h + c]
            bias_ref[c, 0] = jnp.full((tile, tile), cneg, F32)
            bias_ref[c, 4] = jnp.full((tile, tile), cpos, F32)
            for di in range(3):
                for kj in range(n_sub):
                    for qi in range(n_sub):
                        d = (di - 1) * n_sub + kj - qi
                        if -1 <= d <= 1:
                            sub = btile_ref[c, d + 1]
                        else:
                            sub = jnp.full((LANES, LANES), cneg if d < 0 else cpos, F32)
                        bias_ref[c, di + 1, kj * LANES:(kj + 1) * LANES, qi * LANES:(qi + 1) * LANES] = sub

    m_ref[...] = jnp.full(m_ref.shape, -jnp.inf, F32)
    acc_ref[...] = jnp.zeros(acc_ref.shape, F32)
    q = q_ref[...] * (ATT_QK_DIM ** -0.5)
    lane = lax.broadcasted_iota(jnp.int32, q.shape, 1)
    zero = jnp.zeros_like(q)
    q_maps = (jnp.where(lane < ATT_QK_DIM, q, zero), jnp.where(lane >= ATT_QK_DIM, q, zero))

    def scores(j, s_ref, smax_ref):
        kt = k_ref[pl.ds(pl.multiple_of(j * tile, tile), tile), :]
        bidx = jnp.clip(j - i, -2, 2) + 2
        for c in range(2):
            s = lax.dot_general(kt, q_maps[c], (((1,), (1,)), ((), ())), preferred_element_type=F32)
            s = s + bias_ref[c, bidx]
            s_ref[c] = s
            smax_ref[c] = jnp.max(s, axis=0, keepdims=True)

    def accumulate(j, s_ref, smax_ref):
        vt = vt_ref[j]
        for c in range(2):
            m_old = m_ref[c]
            m_new = jnp.maximum(m_old, smax_ref[c])
            alpha = jnp.exp(m_old - m_new)
            e = jnp.exp(s_ref[c] - m_new).astype(BF16)
            acc_ref[c] = alpha * acc_ref[c] + jnp.dot(vt, e, preferred_element_type=F32)
            m_ref[c] = m_new

    scores(0, sa_ref, ma_ref)

    def body(t, carry):
        j = 2 * t
        scores(j + 1, sb_ref, mb_ref)
        accumulate(j, sa_ref, ma_ref)
        scores(j + 2, sa_ref, ma_ref)
        accumulate(j + 1, sb_ref, mb_ref)
        return carry

    lax.fori_loop(0, n_tiles // 2 - 1, body, 0)
    scores(n_tiles - 1, sb_ref, mb_ref)
    accumulate(n_tiles - 2, sa_ref, ma_ref)
    accumulate(n_tiles - 1, sb_ref, mb_ref)

    lam_v = lam_ref[...]
    s1 = jnp.sum(lam_v[0:1] * lam_v[1:2], axis=-1, keepdims=True)
    s2 = jnp.sum(lam_v[2:3] * lam_v[3:4], axis=-1, keepdims=True)
    lam = jnp.exp(s1) - jnp.exp(s2) + LAMBDA_INIT
    outs = []
    for c in range(2):
        acc = acc_ref[c]
        outs.append(acc[:ATT_V_DIM] / acc[ATT_V_DIM:ATT_V_DIM + 1])
    o_t = outs[0] - lam * outs[1]
    y_t = o_t * lax.rsqrt(jnp.mean(o_t * o_t, axis=0, keepdims=True) + EPS)
    o_ref[...] = (y_t.T * hg_ref[...] * (1.0 - LAMBDA_INIT)).astype(BF16)


def _attention(proj, btiles, rel_bias, lam_vecs, head_g, *, tile):
    nb, seq, _ = proj.shape
    assert seq % (2 * tile) == 0 and tile % LANES == 0
    btiles = btiles.reshape(ATT_HEADS, 2, 3, LANES, LANES)
    return pl.pallas_call(
        functools.partial(_attn_body, tile=tile),
        out_shape=jax.ShapeDtypeStruct((nb, seq, ATT_WIDTH), BF16),
        grid=(nb, ATT_HEADS, seq // tile),
        in_specs=[
            pl.BlockSpec(memory_space=pltpu.SMEM),
            pl.BlockSpec((4, ATT_QK_DIM), lambda b, h, i: (0, 0)),
            pl.BlockSpec((None, tile, LANES), lambda b, h, i: (b, i, h)),
            pl.BlockSpec((None, seq, LANES), lambda b, h, i: (b, 0, ATT_HEADS + h)),
            pl.BlockSpec((None, seq, LANES), lambda b, h, i: (b, 0, 2 * ATT_HEADS + h)),
            pl.BlockSpec((None, 2, 3, LANES, LANES), lambda b, h, i: (h, 0, 0, 0, 0)),
            pl.BlockSpec((1, ATT_V_DIM), lambda b, h, i: (0, h)),
        ],
        out_specs=pl.BlockSpec((None, tile, ATT_V_DIM), lambda b, h, i: (b, i, h)),
        scratch_shapes=[
            pltpu.VMEM((seq // tile, V_ROWS, tile), BF16),
            pltpu.VMEM((2, 5, tile, tile), F32),
            pltpu.VMEM((2, tile, tile), F32),
            pltpu.VMEM((2, tile, tile), F32),
            pltpu.VMEM((2, 1, tile), F32),
            pltpu.VMEM((2, 1, tile), F32),
            pltpu.VMEM((2, V_ROWS, tile), F32),
            pltpu.VMEM((2, 1, tile), F32),
        ],
        compiler_params=_params(3),
        name="diff_attention",
    )(rel_bias, lam_vecs, proj, proj, proj, btiles, head_g)


def _log_sigmoid(x):
    return jnp.minimum(x, 0.0) - jnp.log1p(jnp.exp(-jnp.abs(x)))


def _lane_scan(x, reverse):
    lane = lax.broadcasted_iota(jnp.int32, x.shape, 1)
    sh = 1
    while sh < LANES:
        if reverse:
            x = x + jnp.where(lane < LANES - sh, pltpu.roll(x, LANES - sh, 1), 0.0)
        else:
            x = x + jnp.where(lane >= sh, pltpu.roll(x, sh, 1), 0.0)
        sh *= 2
    return x


def _mlstm_body(mq_ref, mk_ref, mv_ref, mo_ref, cwq_ref, cwk_ref, cbq_ref, cbk_ref, gates_ref, hg_ref,
                o_ref, q_s, kt_s, gsc, hbuf, c_s):
    seq, dh = mq_ref.shape
    L = MLSTM_CHUNK
    nc = seq // L
    halo = 16

    def conv_chunk(src_ref, w_ref, b_ref, c):
        start = pl.multiple_of(c * L, L)
        cur = src_ref[pl.ds(start, L), :].astype(F32)
        prev_start = pl.multiple_of(jnp.maximum(start - halo, 0), halo)
        next_start = pl.multiple_of(jnp.minimum(start + L, seq - halo), halo)
        prev = src_ref[pl.ds(prev_start, halo), :].astype(F32) * jnp.where(c > 0, 1.0, 0.0)
        nxt = src_ref[pl.ds(next_start, halo), :].astype(F32) * jnp.where(c < nc - 1, 1.0, 0.0)
        ext = jnp.concatenate([prev, cur, nxt], axis=0)
        w = w_ref[...]
        out = jnp.broadcast_to(b_ref[...], (L, dh))
        pad = CONV_WIDTH // 2
        for t in range(CONV_WIDTH):
            lo = halo + t - pad
            out = out + ext[lo:lo + L, :] * w[t:t + 1, :]
        return out * jax.nn.sigmoid(out)

    def prep(c, carry):
        start = pl.multiple_of(c * L, L)
        qc = conv_chunk(mq_ref, cwq_ref, cbq_ref, c) * (MLSTM_HEAD_DIM ** -0.5)
        q_s[pl.ds(start, L), :] = qc.astype(BF16)
        kc = conv_chunk(mk_ref, cwk_ref, cbk_ref, c)
        kt_s[c] = kc.T.astype(BF16)
        return carry

    lax.fori_loop(0, nc, prep, 0)

    g = gates_ref[...]
    b_f = _lane_scan(_log_sigmoid(g[2]), reverse=False)
    b_b = _lane_scan(_log_sigmoid(g[3]), reverse=True)
    gsc[0] = b_f
    gsc[1] = g[0] - b_f
    gsc[2] = b_b
    gsc[3] = g[1] - b_b

    c_s[...] = jnp.zeros(c_s.shape, F32)

    row_i = lax.broadcasted_iota(jnp.int32, (L, L), 0)
    col_i = lax.broadcasted_iota(jnp.int32, (L, L), 1)
    lane_row = lax.broadcasted_iota(jnp.int32, (1, L), 1)
    ones_col = jnp.where(lax.broadcasted_iota(jnp.int32, (L, LANES), 1) == 0, 1.0, 0.0).astype(BF16)

    def chunk_step(c, m, direction):
        start = pl.multiple_of(c * L, L)
        causal = (col_i <= row_i) if direction == 0 else (col_i >= row_i)
        last_lane = L - 1 if direction == 0 else 0
        q = q_s[pl.ds(start, L), :]
        kt = kt_s[c]
        vaug = jnp.concatenate([mv_ref[pl.ds(start, L), :], ones_col], axis=1)
        brow = gsc[2 * direction, pl.ds(c, 1), :]
        rrow = gsc[2 * direction + 1, pl.ds(c, 1), :]
        bcol = jnp.sum(jnp.where(row_i == col_i, brow, 0.0), axis=1, keepdims=True)
        dmat = jnp.where(causal, bcol + rrow, -jnp.inf)
        dmax = jnp.max(dmat, axis=1, keepdims=True)
        qk = jnp.dot(q, kt, preferred_element_type=F32)
        st = (qk * jnp.exp(dmat - dmax)).astype(BF16)
        p_intra = jnp.dot(st, vaug, preferred_element_type=F32)
        c_old = c_s[direction]
        p_inter = jnp.dot(q, c_old.astype(BF16), preferred_element_type=F32)
        inter = bcol + m
        m_t = jnp.maximum(inter, dmax)
        num_aug = jnp.exp(inter - m_t) * p_inter + jnp.exp(dmax - m_t) * p_intra
        den = num_aug[:, dh:dh + 1]
        hout = num_aug[:, :dh] / jnp.maximum(jnp.abs(den), jnp.exp(-m_t))
        b_last = jnp.sum(jnp.where(lane_row == last_lane, brow, 0.0), axis=1, keepdims=True)
        grow = b_last + rrow
        gmax = jnp.max(grow, axis=1, keepdims=True)
        m_new = jnp.maximum(b_last + m, gmax)
        wk = jnp.exp(grow - gmax)
        ktw = (kt.astype(F32) * wk).astype(BF16)
        upd = jnp.dot(ktw, vaug, preferred_element_type=F32)
        c_s[direction] = jnp.exp(b_last + m - m_new) * c_old + jnp.exp(gmax - m_new) * upd
        return hout, m_new

    def finalize(c, hm):
        start = pl.multiple_of(c * L, L)
        y = _rms(hm, hg_ref[...])
        og = jax.nn.sigmoid(mo_ref[pl.ds(start, L), :].astype(F32))
        o_ref[pl.ds(start, L), :] = (og * y).astype(BF16)

    def first_half(k, carry):
        m_f, m_b = carry
        c_f, c_b = k, nc - 1 - k
        h_f, m_f = chunk_step(c_f, m_f, 0)
        h_b, m_b = chunk_step(c_b, m_b, 1)
        hbuf[pl.ds(pl.multiple_of(c_f * L, L), L), :] = h_f
        hbuf[pl.ds(pl.multiple_of(c_b * L, L), L), :] = h_b
        return m_f, m_b

    def second_half(k, carry):
        m_f, m_b = carry
        c_f, c_b = k, nc - 1 - k
        h_f, m_f = chunk_step(c_f, m_f, 0)
        h_b, m_b = chunk_step(c_b, m_b, 1)
        finalize(c_f, h_f + hbuf[pl.ds(pl.multiple_of(c_f * L, L), L), :])
        finalize(c_b, h_b + hbuf[pl.ds(pl.multiple_of(c_b * L, L), L), :])
        return m_f, m_b

    m0 = jnp.zeros((1, 1), F32)
    carry = lax.fori_loop(0, nc // 2, first_half, (m0, m0))
    lax.fori_loop(nc // 2, nc, second_half, carry)


def _mlstm(proj, gates, conv_w, conv_b, head_g):
    nb, seq, _ = proj.shape
    dh = MLSTM_HEAD_DIM
    nc = seq // MLSTM_CHUNK
    assert seq % MLSTM_CHUNK == 0 and nc % 2 == 0
    base = 3 * ATT_WIDTH // dh
    nh = MLSTM_HEADS

    def col(group):
        return pl.BlockSpec((None, seq, dh), lambda b, h: (b, 0, base + group * nh + h))

    return pl.pallas_call(
        _mlstm_body,
        out_shape=jax.ShapeDtypeStruct((nb, seq, MLSTM_WIDTH), BF16),
        grid=(nb, nh),
        in_specs=[
            col(0), col(1), col(2), col(3),
            pl.BlockSpec((CONV_WIDTH, dh), lambda b, h: (0, h)),
            pl.BlockSpec((CONV_WIDTH, dh), lambda b, h: (0, nh + h)),
            pl.BlockSpec((1, dh), lambda b, h: (0, h)),
            pl.BlockSpec((1, dh), lambda b, h: (0, nh + h)),
            pl.BlockSpec((None, None, 4, nc, MLSTM_CHUNK), lambda b, h: (b, h, 0, 0, 0)),
            pl.BlockSpec((1, dh), lambda b, h: (0, h)),
        ],
        out_specs=pl.BlockSpec((None, seq, dh), lambda b, h: (b, 0, h)),
        scratch_shapes=[
            pltpu.VMEM((seq, dh), BF16),
            pltpu.VMEM((nc, dh, MLSTM_CHUNK), BF16),
            pltpu.VMEM((4, nc, MLSTM_CHUNK), F32),
            pltpu.VMEM((seq, dh), F32),
            pltpu.VMEM((2, dh, dh + LANES), F32),
        ],
        compiler_params=_params(2),
        name="mlstm",
    )(proj, proj, proj, proj, conv_w, conv_w, conv_b, conv_b, gates, head_g)


def _outproj_body(att_ref, ml_ref, wa_ref, wm_ref, x_ref, g_ref, o_ref):
    mixed = (jnp.dot(att_ref[...], wa_ref[...], preferred_element_type=F32)
             + jnp.dot(ml_ref[...], wm_ref[...], preferred_element_type=F32))
    o_ref[...] = x_ref[...] + _rms(mixed, g_ref[...])


def _out_proj(att, ml, w_out, x, g, *, tm):
    t, d = x.shape
    wa = att.shape[1]
    wm = ml.shape[1]
    assert wa == wm
    return pl.pallas_call(
        _outproj_body,
        out_shape=jax.ShapeDtypeStruct((t, d), F32),
        grid=(t // tm,),
        in_specs=[
            pl.BlockSpec((tm, wa), lambda i: (i, 0)),
            pl.BlockSpec((tm, wm), lambda i: (i, 0)),
            pl.BlockSpec((wa, d), lambda i: (0, 0)),
            pl.BlockSpec((wm, d), lambda i: (1, 0)),
            pl.BlockSpec((tm, d), lambda i: (i, 0)),
            pl.BlockSpec((1, d), lambda i: (0, 0)),
        ],
        out_specs=pl.BlockSpec((tm, d), lambda i: (i, 0)),
        compiler_params=_params(1),
        name="out_proj",
    )(att, ml, w_out, w_out, x, g)


def _layer(x, rel_bias, ffn1_pre_g, ffn1_post_g, ffn1_w_gate, ffn1_w_up, ffn1_w_down,
           mix_pre_g, mix_post_g, w_in, gate_bias, conv_w, conv_b, lam_vecs,
           att_head_g, mlstm_head_g, w_out,
           ffn2_pre_g, ffn2_post_g, ffn2_w_gate, ffn2_w_up, ffn2_w_down,
           *, tm, tf, tn, attn_tile):
    nb, seq, d = x.shape
    t = nb * seq
    row = lambda v: v.reshape(1, -1).astype(F32)
    xt = x.reshape(t, d)

    x1 = _ffn(xt, row(ffn1_pre_g), row(ffn1_post_g), ffn1_w_gate.astype(BF16),
              ffn1_w_up.astype(BF16), ffn1_w_down.astype(BF16), tm=tm, tf=tf)

    w_main = w_in[:, :MAIN_COLS].astype(BF16)
    w_gate = jnp.pad(w_in[:, MAIN_COLS:], ((0, 0), (0, LANES - N_GATES))).astype(BF16)
    gbias = jnp.pad(row(gate_bias), ((0, 0), (0, LANES - N_GATES)))
    proj, gates = _in_proj(x1, row(mix_pre_g), w_main, w_gate, gbias, tm=tm, tn=tn)
    proj = proj.reshape(nb, seq, MAIN_COLS)
    nc = seq // MLSTM_CHUNK
    gates = gates[:, :N_GATES].reshape(nb, nc, MLSTM_CHUNK, 4, MLSTM_HEADS).transpose(0, 4, 3, 1, 2)

    btiles = _bias_tiles(rel_bias.astype(F32))
    att = _attention(proj, btiles, rel_bias.astype(F32), lam_vecs, row(att_head_g), tile=attn_tile)
    ml = _mlstm(proj, gates, conv_w.astype(F32), row(conv_b), row(mlstm_head_g))

    x2 = _out_proj(att.reshape(t, ATT_WIDTH), ml.reshape(t, MLSTM_WIDTH), w_out.astype(BF16),
                   x1, row(mix_post_g), tm=tm)
    x3 = _ffn(x2, row(ffn2_pre_g), row(ffn2_post_g), ffn2_w_gate.astype(BF16),
              ffn2_w_up.astype(BF16), ffn2_w_down.astype(BF16), tm=tm, tf=tf)
    return x3.reshape(nb, seq, d)


def kernel(x_prompt, x_sample, rel_bias, ffn1_pre_g, ffn1_post_g, ffn1_w_gate, ffn1_w_up, ffn1_w_down, mix_pre_g, mix_post_g, w_in, gate_bias, conv_w, conv_b, lambda_q1, lambda_k1, lambda_q2, lambda_k2, att_head_g, mlstm_head_g, w_out, ffn2_pre_g, ffn2_post_g, ffn2_w_gate, ffn2_w_up, ffn2_w_down, *, tm=512, tf=512, tn=512, attn_tile=512):
    assert x_prompt.shape[1:] == x_sample.shape[1:]
    assert ffn1_pre_g.shape[0] == 1, "single-layer trunk"
    nbp = x_prompt.shape[0]
    x = jnp.concatenate([x_prompt, x_sample], axis=0)
    lam_vecs = jnp.concatenate([lambda_q1, lambda_k1, lambda_q2, lambda_k2], axis=0).astype(F32)
    y = _layer(x, rel_bias, ffn1_pre_g[0], ffn1_post_g[0], ffn1_w_gate[0], ffn1_w_up[0], ffn1_w_down[0],
               mix_pre_g[0], mix_post_g[0], w_in[0], gate_bias[0], conv_w[0], conv_b[0], lam_vecs,
               att_head_g[0], mlstm_head_g[0], w_out[0],
               ffn2_pre_g[0], ffn2_post_g[0], ffn2_w_gate[0], ffn2_w_up[0], ffn2_w_down[0],
               tm=tm, tf=tf, tn=tn, attn_tile=attn_tile)
    return (y[:nbp], y[nbp:])
```

```python
import functools
import math

import jax
import jax.numpy as jnp
from jax import lax
from jax.experimental import pallas as pl
from jax.experimental.pallas import tpu as pltpu

F32 = jnp.float32
BF16 = jnp.bfloat16

EPS = 1e-6
ATT_HEADS = 8
ATT_QK_DIM = 64
ATT_V_DIM = 128
ATT_WIDTH = ATT_HEADS * ATT_V_DIM
MLSTM_HEADS = 4
MLSTM_HEAD_DIM = 256
MLSTM_WIDTH = MLSTM_HEADS * MLSTM_HEAD_DIM
MLSTM_CHUNK = 128
CONV_WIDTH = 5
N_GATES = 4 * MLSTM_HEADS
MAIN_COLS = 3 * ATT_WIDTH + 4 * MLSTM_WIDTH
REL_BUCKETS = 32
REL_MAX_DIST = 128
LAMBDA_INIT = 0.8 - 0.6 * math.exp(-0.3 * 0)
LOG2E = math.log2(math.e)

LANES = 128
VMEM_LIMIT = 56 * 1024 * 1024


def _params(n_axes):
    return pltpu.CompilerParams(dimension_semantics=("arbitrary",) * n_axes,
                                vmem_limit_bytes=VMEM_LIMIT)


def _rms(xf, g_row):
    ms = jnp.mean(xf * xf, axis=-1, keepdims=True)
    return xf * lax.rsqrt(ms + EPS) * g_row


def _ffn_body(x_ref, pre_g_ref, post_g_ref, wg_ref, wu_ref, wd_ref, o_ref, h_ref, acc_ref):
    f = pl.program_id(1)

    @pl.when(f == 0)
    def _():
        h_ref[...] = _rms(x_ref[...], pre_g_ref[...]).astype(BF16)
        acc_ref[...] = jnp.zeros(acc_ref.shape, F32)

    h = h_ref[...]
    g = jnp.dot(h, wg_ref[...], preferred_element_type=F32)
    u = jnp.dot(h, wu_ref[...], preferred_element_type=F32)
    a = (g * jax.nn.sigmoid(g) * u).astype(BF16)
    acc_ref[...] += jnp.dot(a, wd_ref[...], preferred_element_type=F32)

    @pl.when(f == pl.num_programs(1) - 1)
    def _():
        o_ref[...] = x_ref[...] + 0.5 * _rms(acc_ref[...], post_g_ref[...])


def _ffn(x, pre_g, post_g, wg, wu, wd, *, tm, tf):
    t, d = x.shape
    fdim = wg.shape[1]
    return pl.pallas_call(
        _ffn_body,
        out_shape=jax.ShapeDtypeStruct((t, d), F32),
        grid=(t // tm, fdim // tf),
        in_specs=[
            pl.BlockSpec((tm, d), lambda i, f: (i, 0)),
            pl.BlockSpec((1, d), lambda i, f: (0, 0)),
            pl.BlockSpec((1, d), lambda i, f: (0, 0)),
            pl.BlockSpec((d, tf), lambda i, f: (0, f)),
            pl.BlockSpec((d, tf), lambda i, f: (0, f)),
            pl.BlockSpec((tf, d), lambda i, f: (f, 0)),
        ],
        out_specs=pl.BlockSpec((tm, d), lambda i, f: (i, 0)),
        scratch_shapes=[pltpu.VMEM((tm, d), BF16), pltpu.VMEM((tm, d), F32)],
        compiler_params=_params(2),
        name="ffn",
    )(x, pre_g, post_g, wg, wu, wd)


def _inproj_body(x_ref, g_ref, w_ref, cs_ref, wgate_ref, gbias_ref, o_ref, gates_ref, h_ref):
    j = pl.program_id(1)

    @pl.when(j == 0)
    def _():
        h = _rms(x_ref[...], g_ref[...]).astype(BF16)
        h_ref[...] = h
        gates_ref[...] = jnp.dot(h, wgate_ref[...], preferred_element_type=F32) + gbias_ref[...]

    o_ref[...] = (jnp.dot(h_ref[...], w_ref[...], preferred_element_type=F32) * cs_ref[...]).astype(BF16)


def _in_proj(x, g, w_main, col_scale, w_gate, gate_bias, *, tm, tn):
    t, d = x.shape
    n = w_main.shape[1]
    return pl.pallas_call(
        _inproj_body,
        out_shape=(jax.ShapeDtypeStruct((t, n), BF16), jax.ShapeDtypeStruct((t, LANES), F32)),
        grid=(t // tm, n // tn),
        in_specs=[
            pl.BlockSpec((tm, d), lambda i, j: (i, 0)),
            pl.BlockSpec((1, d), lambda i, j: (0, 0)),
            pl.BlockSpec((d, tn), lambda i, j: (0, j)),
            pl.BlockSpec((1, tn), lambda i, j: (0, j)),
            pl.BlockSpec((d, LANES), lambda i, j: (0, 0)),
            pl.BlockSpec((1, LANES), lambda i, j: (0, 0)),
        ],
        out_specs=(pl.BlockSpec((tm, tn), lambda i, j: (i, j)),
                   pl.BlockSpec((tm, LANES), lambda i, j: (i, 0))),
        scratch_shapes=[pltpu.VMEM((tm, d), BF16)],
        compiler_params=_params(2),
        name="in_proj",
    )(x, g, w_main, col_scale, w_gate, gate_bias)


def _rel_bucket(rel):
    nb = REL_BUCKETS // 2
    max_exact = nb // 2
    ret = jnp.where(rel > 0, nb, 0)
    n = jnp.abs(rel)
    nf = jnp.maximum(n, 1).astype(jnp.float32)
    large = max_exact + (jnp.log(nf / max_exact) / math.log(REL_MAX_DIST / max_exact)
                         * (nb - max_exact)).astype(jnp.int32)
    large = jnp.minimum(large, nb - 1)
    return ret + jnp.where(n < max_exact, n, large)


def _bias_tiles_body(relb_ref, bucket_ref, o_ref):
    hc = pl.program_id(0)
    bucket = bucket_ref[...]
    acc = jnp.zeros(bucket.shape, F32)
    for b in range(REL_BUCKETS):
        acc = jnp.where(bucket == b, relb_ref[b, hc], acc)
    o_ref[...] = acc * LOG2E


def _bias_tiles(rel_bias):
    key = lax.broadcasted_iota(jnp.int32, (3, LANES, LANES), 1)
    query = lax.broadcasted_iota(jnp.int32, (3, LANES, LANES), 2)
    off = (lax.broadcasted_iota(jnp.int32, (3, LANES, LANES), 0) - 1) * LANES
    bucket = _rel_bucket(off + key - query).astype(jnp.int32)
    n_hc = rel_bias.shape[1]
    return pl.pallas_call(
        _bias_tiles_body,
        out_shape=jax.ShapeDtypeStruct((n_hc, 3, LANES, LANES), F32),
        grid=(n_hc,),
        in_specs=[
            pl.BlockSpec(memory_space=pltpu.SMEM),
            pl.BlockSpec((3, LANES, LANES), lambda i: (0, 0, 0)),
        ],
        out_specs=pl.BlockSpec((None, 3, LANES, LANES), lambda i: (i, 0, 0, 0)),
        compiler_params=_params(1),
        name="rel_bias_tiles",
    )(rel_bias, bucket)


V_ROWS = ATT_V_DIM + 16


def _attn_body(relb_ref, lam_ref, q_ref, k_ref, v_ref, btile_ref, hg_ref, o_ref,
               qz_ref, vt_ref, bias_ref, sa_ref, sb_ref, ma_ref, mb_ref, acc_ref, m_ref, *, tile):
    h = pl.program_id(1)
    seq = k_ref.shape[0]
    n_tiles = seq // tile
    n_sub = tile // LANES
    far_bucket_neg = REL_BUCKETS // 2 - 1
    far_bucket_pos = REL_BUCKETS - 1

    q = q_ref[...]
    lane = lax.broadcasted_iota(jnp.int32, q.shape, 1)
    zero = jnp.zeros_like(q)
    qz_ref[0] = jnp.where(lane < ATT_QK_DIM, q, zero)
    qz_ref[1] = jnp.where(lane >= ATT_QK_DIM, q, zero)
    ones_rows = jnp.where(lax.broadcasted_iota(jnp.int32, (V_ROWS - ATT_V_DIM, tile), 0) == 0,
                          1.0, 0.0).astype(BF16)
    for jt in range(n_tiles):
        vt_ref[jt, :ATT_V_DIM, :] = v_ref[jt * tile:(jt + 1) * tile, :].T
        vt_ref[jt, ATT_V_DIM:, :] = ones_rows
    for c in range(2):
        cneg = relb_ref[far_bucket_neg, 2 * h + c] * LOG2E
        cpos = relb_ref[far_bucket_pos, 2 * h + c] * LOG2E
        bias_ref[c, 0] = jnp.full((tile, tile), cneg, F32)
        bias_ref[c, 4] = jnp.full((tile, tile), cpos, F32)
        for di in range(3):
            for kj in range(n_sub):
                for qi in range(n_sub):
                    d = (di - 1) * n_sub + kj - qi
                    if -1 <= d <= 1:
                        sub = btile_ref[c, d + 1]
                    else:
                        sub = jnp.full((LANES, LANES), cneg if d < 0 else cpos, F32)
                    bias_ref[c, di + 1, kj * LANES:(kj + 1) * LANES, qi * LANES:(qi + 1) * LANES] = sub
    m_ref[...] = jnp.full(m_ref.shape, -jnp.inf, F32)
    acc_ref[...] = jnp.zeros(acc_ref.shape, F32)

    lam_v = lam_ref[...]
    s1 = jnp.sum(lam_v[0:1] * lam_v[1:2], axis=-1, keepdims=True)
    s2 = jnp.sum(lam_v[2:3] * lam_v[3:4], axis=-1, keepdims=True)
    lam = jnp.exp(s1) - jnp.exp(s2) + LAMBDA_INIT

    def scores(qi, j, s_ref, smax_ref):
        kt = k_ref[pl.ds(pl.multiple_of(j * tile, tile), tile), :]
        bidx = jnp.clip(j - qi, -2, 2) + 2
        for c in range(2):
            qz = qz_ref[c, pl.ds(pl.multiple_of(qi * tile, tile), tile), :]
            s = lax.dot_general(kt, qz, (((1,), (1,)), ((), ())), preferred_element_type=F32)
            s = s + bias_ref[c, bidx]
            s_ref[c] = s
            smax_ref[c] = jnp.max(s, axis=0, keepdims=True)

    def accumulate(j, s_ref, smax_ref):
        vt = vt_ref[j]
        for c in range(2):
            m_old = m_ref[c]
            m_new = jnp.maximum(m_old, smax_ref[c])
            alpha = jnp.exp2(m_old - m_new)
            e = jnp.exp2(s_ref[c] - m_new).astype(BF16)
            acc_ref[c] = alpha * acc_ref[c] + jnp.dot(vt, e, preferred_element_type=F32)
            m_ref[c] = m_new

    def finalize(qi):
        outs = []
        for c in range(2):
            acc = acc_ref[c]
            outs.append(acc[:ATT_V_DIM] / acc[ATT_V_DIM:ATT_V_DIM + 1])
        o_t = outs[0] - lam * outs[1]
        y_t = o_t * lax.rsqrt(jnp.mean(o_t * o_t, axis=0, keepdims=True) + EPS)
        o_ref[pl.ds(pl.multiple_of(qi * tile, tile), tile), :] = (
            y_t.T * hg_ref[...] * (1.0 - LAMBDA_INIT)).astype(BF16)
        m_ref[...] = jnp.full(m_ref.shape, -jnp.inf, F32)
        acc_ref[...] = jnp.zeros(acc_ref.shape, F32)

    scores(0, 0, sa_ref, ma_ref)

    def q_body(qi, carry):
        def pair_body(t, c2):
            j = 2 * t
            scores(qi, j + 1, sb_ref, mb_ref)
            accumulate(j, sa_ref, ma_ref)
            scores(qi, j + 2, sa_ref, ma_ref)
            accumulate(j + 1, sb_ref, mb_ref)
            return c2

        lax.fori_loop(0, n_tiles // 2 - 1, pair_body, 0)
        scores(qi, n_tiles - 1, sb_ref, mb_ref)
        accumulate(n_tiles - 2, sa_ref, ma_ref)
        scores(jnp.minimum(qi + 1, n_tiles - 1), 0, sa_ref, ma_ref)
        accumulate(n_tiles - 1, sb_ref, mb_ref)
        finalize(qi)
        return carry

    lax.fori_loop(0, n_tiles, q_body, 0)


def _attention(proj, btiles, rel_bias, lam_vecs, head_g, *, tile):
    nb, seq, _ = proj.shape
    assert seq % (2 * tile) == 0 and tile % LANES == 0
    btiles = btiles.reshape(ATT_HEADS, 2, 3, LANES, LANES)
    return pl.pallas_call(
        functools.partial(_attn_body, tile=tile),
        out_shape=jax.ShapeDtypeStruct((nb, seq, ATT_WIDTH), BF16),
        grid=(nb, ATT_HEADS),
        in_specs=[
            pl.BlockSpec(memory_space=pltpu.SMEM),
            pl.BlockSpec((4, ATT_QK_DIM), lambda b, h: (0, 0)),
            pl.BlockSpec((None, seq, LANES), lambda b, h: (b, 0, h)),
            pl.BlockSpec((None, seq, LANES), lambda b, h: (b, 0, ATT_HEADS + h)),
            pl.BlockSpec((None, seq, LANES), lambda b, h: (b, 0, 2 * ATT_HEADS + h)),
            pl.BlockSpec((None, 2, 3, LANES, LANES), lambda b, h: (h, 0, 0, 0, 0)),
            pl.BlockSpec((1, ATT_V_DIM), lambda b, h: (0, h)),
        ],
        out_specs=pl.BlockSpec((None, seq, ATT_V_DIM), lambda b, h: (b, 0, h)),
        scratch_shapes=[
            pltpu.VMEM((2, seq, LANES), BF16),
            pltpu.VMEM((seq // tile, V_ROWS, tile), BF16),
            pltpu.VMEM((2, 5, tile, tile), F32),
            pltpu.VMEM((2, tile, tile), F32),
            pltpu.VMEM((2, tile, tile), F32),
            pltpu.VMEM((2, 1, tile), F32),
            pltpu.VMEM((2, 1, tile), F32),
            pltpu.VMEM((2, V_ROWS, tile), F32),
            pltpu.VMEM((2, 1, tile), F32),
        ],
        compiler_params=_params(2),
        name="diff_attention",
    )(rel_bias, lam_vecs, proj, proj, proj, btiles, head_g)


def _log_sigmoid(x):
    return jnp.minimum(x, 0.0) - jnp.log1p(jnp.exp(-jnp.abs(x)))


def _lane_scan(x, reverse):
    lane = lax.broadcasted_iota(jnp.int32, x.shape, 1)
    sh = 1
    while sh < LANES:
        if reverse:
            x = x + jnp.where(lane < LANES - sh, pltpu.roll(x, LANES - sh, 1), 0.0)
        else:
            x = x + jnp.where(lane >= sh, pltpu.roll(x, sh, 1), 0.0)
        sh *= 2
    return x


def _mlstm_body(mq_ref, mk_ref, mv_ref, mo_ref, cwq_ref, cwk_ref, cbq_ref, cbk_ref, gates_ref, hg_ref,
                o_ref, q_s, kt_s, gsc, hbuf, c_s):
    seq, dh = mq_ref.shape
    L = MLSTM_CHUNK
    nc = seq // L
    halo = 16

    def conv_chunk(src_ref, w_ref, b_ref, c):
        start = pl.multiple_of(c * L, L)
        cur = src_ref[pl.ds(start, L), :].astype(F32)
        prev_start = pl.multiple_of(jnp.maximum(start - halo, 0), halo)
        next_start = pl.multiple_of(jnp.minimum(start + L, seq - halo), halo)
        prev = src_ref[pl.ds(prev_start, halo), :].astype(F32) * jnp.where(c > 0, 1.0, 0.0)
        nxt = src_ref[pl.ds(next_start, halo), :].astype(F32) * jnp.where(c < nc - 1, 1.0, 0.0)
        ext = jnp.concatenate([prev, cur, nxt], axis=0)
        w = w_ref[...]
        out = jnp.broadcast_to(b_ref[...], (L, dh))
        pad = CONV_WIDTH // 2
        for t in range(CONV_WIDTH):
            lo = halo + t - pad
            out = out + ext[lo:lo + L, :] * w[t:t + 1, :]
        return out * jax.nn.sigmoid(out)

    def prep(c, carry):
        start = pl.multiple_of(c * L, L)
        qc = conv_chunk(mq_ref, cwq_ref, cbq_ref, c) * (MLSTM_HEAD_DIM ** -0.5)
        q_s[pl.ds(start, L), :] = qc.astype(BF16)
        kc = conv_chunk(mk_ref, cwk_ref, cbk_ref, c)
        kt_s[c] = kc.T.astype(BF16)
        return carry

    lax.fori_loop(0, nc, prep, 0)

    g = gates_ref[...]
    b_f = _lane_scan(_log_sigmoid(g[2]), reverse=False)
    b_b = _lane_scan(_log_sigmoid(g[3]), reverse=True)
    gsc[0] = b_f
    gsc[1] = g[0] - b_f
    gsc[2] = b_b
    gsc[3] = g[1] - b_b

    c_s[...] = jnp.zeros(c_s.shape, F32)

    row_i = lax.broadcasted_iota(jnp.int32, (L, L), 0)
    col_i = lax.broadcasted_iota(jnp.int32, (L, L), 1)
    lane_row = lax.broadcasted_iota(jnp.int32, (1, L), 1)
    ones_col = jnp.where(lax.broadcasted_iota(jnp.int32, (L, LANES), 1) == 0, 1.0, 0.0).astype(BF16)

    def chunk_step(c, m, direction):
        start = pl.multiple_of(c * L, L)
        causal = (col_i <= row_i) if direction == 0 else (col_i >= row_i)
        last_lane = L - 1 if direction == 0 else 0
        q = q_s[pl.ds(start, L), :]
        kt = kt_s[c]
        vaug = jnp.concatenate([mv_ref[pl.ds(start, L), :], ones_col], axis=1)
        brow = gsc[2 * direction, pl.ds(c, 1), :]
        rrow = gsc[2 * direction + 1, pl.ds(c, 1), :]
        bcol = jnp.sum(jnp.where(row_i == col_i, brow, 0.0), axis=1, keepdims=True)
        dmat = jnp.where(causal, bcol + rrow, -jnp.inf)
        dmax = jnp.max(dmat, axis=1, keepdims=True)
        qk = jnp.dot(q, kt, preferred_element_type=F32)
        st = (qk * jnp.exp(dmat - dmax)).astype(BF16)
        p_intra = jnp.dot(st, vaug, preferred_element_type=F32)
        c_old = c_s[direction]
        p_inter = jnp.dot(q, c_old.astype(BF16), preferred_element_type=F32)
        inter = bcol + m
        m_t = jnp.maximum(inter, dmax)
        num_aug = jnp.exp(inter - m_t) * p_inter + jnp.exp(dmax - m_t) * p_intra
        den = num_aug[:, dh:dh + 1]
        hout = num_aug[:, :dh] / jnp.maximum(jnp.abs(den), jnp.exp(-m_t))
        b_last = jnp.sum(jnp.where(lane_row == last_lane, brow, 0.0), axis=1, keepdims=True)
        grow = b_last + rrow
        gmax = jnp.max(grow, axis=1, keepdims=True)
        m_new = jnp.maximum(b_last + m, gmax)
        wk = jnp.exp(grow - gmax)
        ktw = (kt.astype(F32) * wk).astype(BF16)
        upd = jnp.dot(ktw, vaug, preferred_element_type=F32)
        c_s[direction] = jnp.exp(b_last + m - m_new) * c_old + jnp.exp(gmax - m_new) * upd
        return hout, m_new

    def finalize(c, hm):
        start = pl.multiple_of(c * L, L)
        y = _rms(hm, hg_ref[...])
        og = jax.nn.sigmoid(mo_ref[pl.ds(start, L), :].astype(F32))
        o_ref[pl.ds(start, L), :] = (og * y).astype(BF16)

    def first_half(k, carry):
        m_f, m_b = carry
        c_f, c_b = k, nc - 1 - k
        h_f, m_f = chunk_step(c_f, m_f, 0)
        h_b, m_b = chunk_step(c_b, m_b, 1)
        hbuf[pl.ds(pl.multiple_of(c_f * L, L), L), :] = h_f
        hbuf[pl.ds(pl.multiple_of(c_b * L, L), L), :] = h_b
        return m_f, m_b

    def second_half(k, carry):
        m_f, m_b = carry
        c_f, c_b = k, nc - 1 - k
        h_f, m_f = chunk_step(c_f, m_f, 0)
        h_b, m_b = chunk_step(c_b, m_b, 1)
        finalize(c_f, h_f + hbuf[pl.ds(pl.multiple_of(c_f * L, L), L), :])
        finalize(c_b, h_b + hbuf[pl.ds(pl.multiple_of(c_b * L, L), L), :])
        return m_f, m_b

    m0 = jnp.zeros((1, 1), F32)
    carry = lax.fori_loop(0, nc // 2, first_half, (m0, m0))
    lax.fori_loop(nc // 2, nc, second_half, carry)


def _mlstm(proj, gates, conv_w, conv_b, head_g):
    nb, seq, _ = proj.shape
    dh = MLSTM_HEAD_DIM
    nc = seq // MLSTM_CHUNK
    assert seq % MLSTM_CHUNK == 0 and nc % 2 == 0
    base = 3 * ATT_WIDTH // dh
    nh = MLSTM_HEADS

    def col(group):
        return pl.BlockSpec((None, seq, dh), lambda b, h: (b, 0, base + group * nh + h))

    return pl.pallas_call(
        _mlstm_body,
        out_shape=jax.ShapeDtypeStruct((nb, seq, MLSTM_WIDTH), BF16),
        grid=(nb, nh),
        in_specs=[
            col(0), col(1), col(2), col(3),
            pl.BlockSpec((CONV_WIDTH, dh), lambda b, h: (0, h)),
            pl.BlockSpec((CONV_WIDTH, dh), lambda b, h: (0, nh + h)),
            pl.BlockSpec((1, dh), lambda b, h: (0, h)),
            pl.BlockSpec((1, dh), lambda b, h: (0, nh + h)),
            pl.BlockSpec((None, None, 4, nc, MLSTM_CHUNK), lambda b, h: (b, h, 0, 0, 0)),
            pl.BlockSpec((1, dh), lambda b, h: (0, h)),
        ],
        out_specs=pl.BlockSpec((None, seq, dh), lambda b, h: (b, 0, h)),
        scratch_shapes=[
            pltpu.VMEM((seq, dh), BF16),
            pltpu.VMEM((nc, dh, MLSTM_CHUNK), BF16),
            pltpu.VMEM((4, nc, MLSTM_CHUNK), F32),
            pltpu.VMEM((seq, dh), F32),
            pltpu.VMEM((2, dh, dh + LANES), F32),
        ],
        compiler_params=_params(2),
        name="mlstm",
    )(proj, proj, proj, proj, conv_w, conv_w, conv_b, conv_b, gates, head_g)


def _outproj_body(att_ref, ml_ref, wa_ref, wm_ref, x_ref, g_ref, o_ref):
    mixed = (jnp.dot(att_ref[...], wa_ref[...], preferred_element_type=F32)
             + jnp.dot(ml_ref[...], wm_ref[...], preferred_element_type=F32))
    o_ref[...] = x_ref[...] + _rms(mixed, g_ref[...])


def _out_proj(att, ml, w_out, x, g, *, tm):
    t, d = x.shape
    wa = att.shape[1]
    wm = ml.shape[1]
    assert wa == wm
    return pl.pallas_call(
        _outproj_body,
        out_shape=jax.ShapeDtypeStruct((t, d), F32),
        grid=(t // tm,),
        in_specs=[
            pl.BlockSpec((tm, wa), lambda i: (i, 0)),
            pl.BlockSpec((tm, wm), lambda i: (i, 0)),
            pl.BlockSpec((wa, d), lambda i: (0, 0)),
            pl.BlockSpec((wm, d), lambda i: (1, 0)),
            pl.BlockSpec((tm, d), lambda i: (i, 0)),
            pl.BlockSpec((1, d), lambda i: (0, 0)),
        ],
        out_specs=pl.BlockSpec((tm, d), lambda i: (i, 0)),
        compiler_params=_params(1),
        name="out_proj",
    )(att, ml, w_out, w_out, x, g)


def _layer(x, rel_bias, ffn1_pre_g, ffn1_post_g, ffn1_w_gate, ffn1_w_up, ffn1_w_down,
           mix_pre_g, mix_post_g, w_in, gate_bias, conv_w, conv_b, lam_vecs,
           att_head_g, mlstm_head_g, w_out,
           ffn2_pre_g, ffn2_post_g, ffn2_w_gate, ffn2_w_up, ffn2_w_down,
           *, tm, tf, tn, attn_tile):
    nb, seq, d = x.shape
    t = nb * seq
    row = lambda v: v.reshape(1, -1).astype(F32)
    xt = x.reshape(t, d)

    x1 = _ffn(xt, row(ffn1_pre_g), row(ffn1_post_g), ffn1_w_gate.astype(BF16),
              ffn1_w_up.astype(BF16), ffn1_w_down.astype(BF16), tm=tm, tf=tf)

    w_main = w_in[:, :MAIN_COLS].astype(BF16)
    w_gate = jnp.pad(w_in[:, MAIN_COLS:], ((0, 0), (0, LANES - N_GATES))).astype(BF16)
    gbias = jnp.pad(row(gate_bias), ((0, 0), (0, LANES - N_GATES)))
    col_scale = jnp.concatenate([jnp.full((1, ATT_WIDTH), ATT_QK_DIM ** -0.5 * LOG2E, F32),
                                 jnp.ones((1, MAIN_COLS - ATT_WIDTH), F32)], axis=1)
    proj, gates = _in_proj(x1, row(mix_pre_g), w_main, col_scale, w_gate, gbias, tm=tm, tn=tn)
    proj = proj.reshape(nb, seq, MAIN_COLS)
    nc = seq // MLSTM_CHUNK
    gates = gates[:, :N_GATES].reshape(nb, nc, MLSTM_CHUNK, 4, MLSTM_HEADS).transpose(0, 4, 3, 1, 2)

    btiles = _bias_tiles(rel_bias.astype(F32))
    att = _attention(proj, btiles, rel_bias.astype(F32), lam_vecs, row(att_head_g), tile=attn_tile)
    ml = _mlstm(proj, gates, conv_w.astype(F32), row(conv_b), row(mlstm_head_g))

    x2 = _out_proj(att.reshape(t, ATT_WIDTH), ml.reshape(t, MLSTM_WIDTH), w_out.astype(BF16),
                   x1, row(mix_post_g), tm=tm)
    x3 = _ffn(x2, row(ffn2_pre_g), row(ffn2_post_g), ffn2_w_gate.astype(BF16),
              ffn2_w_up.astype(BF16), ffn2_w_down.astype(BF16), tm=tm, tf=tf)
    return x3.reshape(nb, seq, d)


def kernel(x_prompt, x_sample, rel_bias, ffn1_pre_g, ffn1_post_g, ffn1_w_gate, ffn1_w_up, ffn1_w_down, mix_pre_g, mix_post_g, w_in, gate_bias, conv_w, conv_b, lambda_q1, lambda_k1, lambda_q2, lambda_k2, att_head_g, mlstm_head_g, w_out, ffn2_pre_g, ffn2_post_g, ffn2_w_gate, ffn2_w_up, ffn2_w_down, *, tm=512, tf=512, tn=1792, attn_tile=512):
    assert x_prompt.shape[1:] == x_sample.shape[1:]
    assert ffn1_pre_g.shape[0] == 1, "single-layer trunk"
    nbp = x_prompt.shape[0]
    x = jnp.concatenate([x_prompt, x_sample], axis=0)
    lam_vecs = jnp.concatenate([lambda_q1, lambda_k1, lambda_q2, lambda_k2], axis=0).astype(F32)
    y = _layer(x, rel_bias, ffn1_pre_g[0], ffn1_post_g[0], ffn1_w_gate[0], ffn1_w_up[0], ffn1_w_down[0],
               mix_pre_g[0], mix_post_g[0], w_in[0], gate_bias[0], conv_w[0], conv_b[0], lam_vecs,
               att_head_g[0], mlstm_head_g[0], w_out[0],
               ffn2_pre_g[0], ffn2_post_g[0], ffn2_w_gate[0], ffn2_w_up[0], ffn2_w_down[0],
               tm=tm, tf=tf, tn=tn, attn_tile=attn_tile)
    return (y[:nbp], y[nbp:])
```

```python
import functools
import math

import jax
import jax.numpy as jnp
from jax import lax
from jax.experimental import pallas as pl
from jax.experimental.pallas import tpu as pltpu

F32 = jnp.float32
BF16 = jnp.bfloat16

EPS = 1e-6
ATT_HEADS = 8
ATT_QK_DIM = 64
ATT_V_DIM = 128
ATT_WIDTH = ATT_HEADS * ATT_V_DIM
MLSTM_HEADS = 4
MLSTM_HEAD_DIM = 256
MLSTM_WIDTH = MLSTM_HEADS * MLSTM_HEAD_DIM
MLSTM_CHUNK = 128
CONV_WIDTH = 5
N_GATES = 4 * MLSTM_HEADS
MAIN_COLS = 3 * ATT_WIDTH + 4 * MLSTM_WIDTH
REL_BUCKETS = 32
REL_MAX_DIST = 128
LAMBDA_INIT = 0.8 - 0.6 * math.exp(-0.3 * 0)
LOG2E = math.log2(math.e)

LANES = 128
VMEM_LIMIT = 56 * 1024 * 1024


def _params(n_axes):
    return pltpu.CompilerParams(dimension_semantics=("arbitrary",) * n_axes,
                                vmem_limit_bytes=VMEM_LIMIT)


def _rms(xf, g_row):
    ms = jnp.mean(xf * xf, axis=-1, keepdims=True)
    return xf * lax.rsqrt(ms + EPS) * g_row


def _segment_of(i, tile_starts):
    seg = 0
    for start in tile_starts[1:]:
        seg = seg + (i >= start).astype(jnp.int32)
    return seg


def _ffn_body(*refs, in_starts, out_starts):
    n_in, n_out = len(in_starts), len(out_starts)
    x_refs = refs[:n_in]
    pre_g_ref, post_g_ref, wg_ref, wu_ref, wd_ref = refs[n_in:n_in + 5]
    o_refs = refs[n_in + 5:n_in + 5 + n_out]
    h_ref, acc_ref = refs[n_in + 5 + n_out:]
    i = pl.program_id(0)
    f = pl.program_id(1)
    in_seg = _segment_of(i, in_starts)
    out_seg = _segment_of(i, out_starts)

    for k, x_ref in enumerate(x_refs):
        @pl.when(jnp.logical_and(f == 0, in_seg == k))
        def _(x_ref=x_ref):
            h_ref[...] = _rms(x_ref[...], pre_g_ref[...]).astype(BF16)
            acc_ref[...] = jnp.zeros(acc_ref.shape, F32)

    h = h_ref[...]
    g = jnp.dot(h, wg_ref[...], preferred_element_type=F32)
    u = jnp.dot(h, wu_ref[...], preferred_element_type=F32)
    a = (g * jax.nn.sigmoid(g) * u).astype(BF16)
    acc_ref[...] += jnp.dot(a, wd_ref[...], preferred_element_type=F32)

    last = f == pl.num_programs(1) - 1
    for k, x_ref in enumerate(x_refs):
        for m, o_ref in enumerate(o_refs):
            @pl.when(jnp.logical_and(last, jnp.logical_and(in_seg == k, out_seg == m)))
            def _(x_ref=x_ref, o_ref=o_ref):
                o_ref[...] = x_ref[...] + 0.5 * _rms(acc_ref[...], post_g_ref[...])


def _ffn(xs, pre_g, post_g, wg, wu, wd, out_rows, *, tm, tf):
    d = xs[0].shape[1]
    fdim = wg.shape[1]

    def tile_starts(rows):
        assert all(r % tm == 0 for r in rows)
        starts = [0]
        for r in rows[:-1]:
            starts.append(starts[-1] + r // tm)
        return tuple(starts)

    in_rows = tuple(x.shape[0] for x in xs)
    assert sum(in_rows) == sum(out_rows)
    in_starts, out_starts = tile_starts(in_rows), tile_starts(out_rows)

    def seg_spec(start, rows):
        n = rows // tm
        return pl.BlockSpec((tm, d), lambda i, f: (jnp.clip(i - start, 0, n - 1), 0))

    fixed = lambda i, f: (0, 0)
    return pl.pallas_call(
        functools.partial(_ffn_body, in_starts=in_starts, out_starts=out_starts),
        out_shape=tuple(jax.ShapeDtypeStruct((r, d), F32) for r in out_rows),
        grid=(sum(in_rows) // tm, fdim // tf),
        in_specs=[seg_spec(s, r) for s, r in zip(in_starts, in_rows)] + [
            pl.BlockSpec((1, d), fixed),
            pl.BlockSpec((1, d), fixed),
            pl.BlockSpec((d, tf), lambda i, f: (0, f)),
            pl.BlockSpec((d, tf), lambda i, f: (0, f)),
            pl.BlockSpec((tf, d), lambda i, f: (f, 0)),
        ],
        out_specs=tuple(seg_spec(s, r) for s, r in zip(out_starts, out_rows)),
        scratch_shapes=[pltpu.VMEM((tm, d), BF16), pltpu.VMEM((tm, d), F32)],
        compiler_params=_params(2),
        name="ffn",
    )(*xs, pre_g, post_g, wg, wu, wd)


def _inproj_body(x_ref, g_ref, w_ref, cs_ref, wgate_ref, gbias_ref, o_ref, gates_ref, h_ref):
    j = pl.program_id(1)

    @pl.when(j == 0)
    def _():
        h = _rms(x_ref[...], g_ref[...]).astype(BF16)
        h_ref[...] = h
        gates_ref[...] = jnp.dot(h, wgate_ref[...], preferred_element_type=F32) + gbias_ref[...]

    o_ref[...] = (jnp.dot(h_ref[...], w_ref[...], preferred_element_type=F32) * cs_ref[...]).astype(BF16)


def _in_proj(x, g, w_main, col_scale, w_gate, gate_bias, *, tm, tn):
    t, d = x.shape
    n = col_scale.shape[1]
    assert n % tn == 0 and n <= w_main.shape[1]
    return pl.pallas_call(
        _inproj_body,
        out_shape=(jax.ShapeDtypeStruct((t, n), BF16), jax.ShapeDtypeStruct((t, LANES), F32)),
        grid=(t // tm, n // tn),
        in_specs=[
            pl.BlockSpec((tm, d), lambda i, j: (i, 0)),
            pl.BlockSpec((1, d), lambda i, j: (0, 0)),
            pl.BlockSpec((d, tn), lambda i, j: (0, j)),
            pl.BlockSpec((1, tn), lambda i, j: (0, j)),
            pl.BlockSpec((d, LANES), lambda i, j: (0, 0)),
            pl.BlockSpec((1, LANES), lambda i, j: (0, 0)),
        ],
        out_specs=(pl.BlockSpec((tm, tn), lambda i, j: (i, j)),
                   pl.BlockSpec((tm, LANES), lambda i, j: (i, 0))),
        scratch_shapes=[pltpu.VMEM((tm, d), BF16)],
        compiler_params=_params(2),
        name="in_proj",
    )(x, g, w_main, col_scale, w_gate, gate_bias)


def _rel_bucket(rel):
    nb = REL_BUCKETS // 2
    max_exact = nb // 2
    ret = jnp.where(rel > 0, nb, 0)
    n = jnp.abs(rel)
    nf = jnp.maximum(n, 1).astype(jnp.float32)
    large = max_exact + (jnp.log(nf / max_exact) / math.log(REL_MAX_DIST / max_exact)
                         * (nb - max_exact)).astype(jnp.int32)
    large = jnp.minimum(large, nb - 1)
    return ret + jnp.where(n < max_exact, n, large)


def _bias_tiles_body(relb_ref, bucket_ref, o_ref):
    hc = pl.program_id(0)
    bucket = bucket_ref[...]
    acc = jnp.zeros(bucket.shape, F32)
    for b in range(REL_BUCKETS):
        acc = jnp.where(bucket == b, relb_ref[b, hc], acc)
    o_ref[...] = acc * LOG2E


def _bias_tiles(rel_bias):
    key = lax.broadcasted_iota(jnp.int32, (3, LANES, LANES), 1)
    query = lax.broadcasted_iota(jnp.int32, (3, LANES, LANES), 2)
    off = (lax.broadcasted_iota(jnp.int32, (3, LANES, LANES), 0) - 1) * LANES
    bucket = _rel_bucket(off + key - query).astype(jnp.int32)
    n_hc = rel_bias.shape[1]
    return pl.pallas_call(
        _bias_tiles_body,
        out_shape=jax.ShapeDtypeStruct((n_hc, 3, LANES, LANES), F32),
        grid=(n_hc,),
        in_specs=[
            pl.BlockSpec(memory_space=pltpu.SMEM),
            pl.BlockSpec((3, LANES, LANES), lambda i: (0, 0, 0)),
        ],
        out_specs=pl.BlockSpec((None, 3, LANES, LANES), lambda i: (i, 0, 0, 0)),
        compiler_params=_params(1),
        name="rel_bias_tiles",
    )(rel_bias, bucket)


V_ROWS = ATT_V_DIM + 16


def _attn_body(relb_ref, lam_ref, q_ref, k_ref, v_ref, btile_ref, hg_ref, o_ref,
               qz_ref, vt_ref, bias_ref, sa_ref, sb_ref, ma_ref, mb_ref, acc_ref, m_ref, *, tile):
    h = pl.program_id(1)
    seq = k_ref.shape[0]
    n_tiles = seq // tile
    n_sub = tile // LANES
    far_bucket_neg = REL_BUCKETS // 2 - 1
    far_bucket_pos = REL_BUCKETS - 1

    q = q_ref[...]
    lane = lax.broadcasted_iota(jnp.int32, q.shape, 1)
    zero = jnp.zeros_like(q)
    qz_ref[0] = jnp.where(lane < ATT_QK_DIM, q, zero)
    qz_ref[1] = jnp.where(lane >= ATT_QK_DIM, q, zero)
    ones_rows = jnp.where(lax.broadcasted_iota(jnp.int32, (V_ROWS - ATT_V_DIM, tile), 0) == 0,
                          1.0, 0.0).astype(BF16)
    for jt in range(n_tiles):
        vt_ref[jt, :ATT_V_DIM, :] = v_ref[jt * tile:(jt + 1) * tile, :].T
        vt_ref[jt, ATT_V_DIM:, :] = ones_rows
    for c in range(2):
        cneg = relb_ref[far_bucket_neg, 2 * h + c] * LOG2E
        cpos = relb_ref[far_bucket_pos, 2 * h + c] * LOG2E
        bias_ref[c, 0] = jnp.full((tile, tile), cneg, F32)
        bias_ref[c, 4] = jnp.full((tile, tile), cpos, F32)
        for di in range(3):
            for kj in range(n_sub):
                for qi in range(n_sub):
                    d = (di - 1) * n_sub + kj - qi
                    if -1 <= d <= 1:
                        sub = btile_ref[c, d + 1]
                    else:
                        sub = jnp.full((LANES, LANES), cneg if d < 0 else cpos, F32)
                    bias_ref[c, di + 1, kj * LANES:(kj + 1) * LANES, qi * LANES:(qi + 1) * LANES] = sub
    m_ref[...] = jnp.full(m_ref.shape, -jnp.inf, F32)
    acc_ref[...] = jnp.zeros(acc_ref.shape, F32)

    lam_v = lam_ref[...]
    s1 = jnp.sum(lam_v[0:1] * lam_v[1:2], axis=-1, keepdims=True)
    s2 = jnp.sum(lam_v[2:3] * lam_v[3:4], axis=-1, keepdims=True)
    lam = jnp.exp(s1) - jnp.exp(s2) + LAMBDA_INIT

    def scores(qi, j, s_ref, smax_ref):
        kt = k_ref[pl.ds(pl.multiple_of(j * tile, tile), tile), :]
        bidx = jnp.clip(j - qi, -2, 2) + 2
        for c in range(2):
            qz = qz_ref[c, pl.ds(pl.multiple_of(qi * tile, tile), tile), :]
            s = lax.dot_general(kt, qz, (((1,), (1,)), ((), ())), preferred_element_type=F32)
            s = s + bias_ref[c, bidx]
            s_ref[c] = s
            smax_ref[c] = jnp.max(s, axis=0, keepdims=True)

    def accumulate(j, s_ref, smax_ref):
        vt = vt_ref[j]
        for c in range(2):
            m_old = m_ref[c]
            m_new = jnp.maximum(m_old, smax_ref[c])
            alpha = jnp.exp2(m_old - m_new)
            e = jnp.exp2(s_ref[c] - m_new).astype(BF16)
            acc_ref[c] = alpha * acc_ref[c] + jnp.dot(vt, e, preferred_element_type=F32)
            m_ref[c] = m_new

    def finalize(qi):
        outs = []
        for c in range(2):
            acc = acc_ref[c]
            outs.append(acc[:ATT_V_DIM] / acc[ATT_V_DIM:ATT_V_DIM + 1])
        o_t = outs[0] - lam * outs[1]
        y_t = o_t * lax.rsqrt(jnp.mean(o_t * o_t, axis=0, keepdims=True) + EPS)
        o_ref[pl.ds(pl.multiple_of(qi * tile, tile), tile), :] = (
            y_t.T * hg_ref[...] * (1.0 - LAMBDA_INIT)).astype(BF16)
        m_ref[...] = jnp.full(m_ref.shape, -jnp.inf, F32)
        acc_ref[...] = jnp.zeros(acc_ref.shape, F32)

    scores(0, 0, sa_ref, ma_ref)

    def q_body(qi, carry):
        def pair_body(t, c2):
            j = 2 * t
            scores(qi, j + 1, sb_ref, mb_ref)
            accumulate(j, sa_ref, ma_ref)
            scores(qi, j + 2, sa_ref, ma_ref)
            accumulate(j + 1, sb_ref, mb_ref)
            return c2

        lax.fori_loop(0, n_tiles // 2 - 1, pair_body, 0)
        scores(qi, n_tiles - 1, sb_ref, mb_ref)
        accumulate(n_tiles - 2, sa_ref, ma_ref)
        scores(jnp.minimum(qi + 1, n_tiles - 1), 0, sa_ref, ma_ref)
        accumulate(n_tiles - 1, sb_ref, mb_ref)
        finalize(qi)
        return carry

    lax.fori_loop(0, n_tiles, q_body, 0)


def _attention(proj, btiles, rel_bias, lam_vecs, head_g, *, tile):
    nb, seq, _ = proj.shape
    assert seq % (2 * tile) == 0 and tile % LANES == 0
    btiles = btiles.reshape(ATT_HEADS, 2, 3, LANES, LANES)
    return pl.pallas_call(
        functools.partial(_attn_body, tile=tile),
        out_shape=jax.ShapeDtypeStruct((nb, seq, ATT_WIDTH), BF16),
        grid=(nb, ATT_HEADS),
        in_specs=[
            pl.BlockSpec(memory_space=pltpu.SMEM),
            pl.BlockSpec((4, ATT_QK_DIM), lambda b, h: (0, 0)),
            pl.BlockSpec((None, seq, LANES), lambda b, h: (b, 0, h)),
            pl.BlockSpec((None, seq, LANES), lambda b, h: (b, 0, ATT_HEADS + h)),
            pl.BlockSpec((None, seq, LANES), lambda b, h: (b, 0, 2 * ATT_HEADS + h)),
            pl.BlockSpec((None, 2, 3, LANES, LANES), lambda b, h: (h, 0, 0, 0, 0)),
            pl.BlockSpec((1, ATT_V_DIM), lambda b, h: (0, h)),
        ],
        out_specs=pl.BlockSpec((None, seq, ATT_V_DIM), lambda b, h: (b, 0, h)),
        scratch_shapes=[
            pltpu.VMEM((2, seq, LANES), BF16),
            pltpu.VMEM((seq // tile, V_ROWS, tile), BF16),
            pltpu.VMEM((2, 5, tile, tile), F32),
            pltpu.VMEM((2, tile, tile), F32),
            pltpu.VMEM((2, tile, tile), F32),
            pltpu.VMEM((2, 1, tile), F32),
            pltpu.VMEM((2, 1, tile), F32),
            pltpu.VMEM((2, V_ROWS, tile), F32),
            pltpu.VMEM((2, 1, tile), F32),
        ],
        compiler_params=_params(2),
        name="diff_attention",
    )(rel_bias, lam_vecs, proj, proj, proj, btiles, head_g)


def _log_sigmoid(x):
    return jnp.minimum(x, 0.0) - jnp.log1p(jnp.exp(-jnp.abs(x)))


def _lane_scan(x, reverse):
    lane = lax.broadcasted_iota(jnp.int32, x.shape, 1)
    sh = 1
    while sh < LANES:
        if reverse:
            x = x + jnp.where(lane < LANES - sh, pltpu.roll(x, LANES - sh, 1), 0.0)
        else:
            x = x + jnp.where(lane >= sh, pltpu.roll(x, sh, 1), 0.0)
        sh *= 2
    return x


def _mlstm_body(mq_ref, mk_ref, mv_ref, mo_ref, cwq_ref, cwk_ref, cbq_ref, cbk_ref, gates_ref, hg_ref,
                o_ref, q_s, kt_s, gsc, hbuf, c_s):
    seq, dh = mq_ref.shape
    L = MLSTM_CHUNK
    nc = seq // L
    halo = 16

    def conv_chunk(src_ref, w_ref, b_ref, c):
        start = pl.multiple_of(c * L, L)
        cur = src_ref[pl.ds(start, L), :].astype(F32)
        prev_start = pl.multiple_of(jnp.maximum(start - halo, 0), halo)
        next_start = pl.multiple_of(jnp.minimum(start + L, seq - halo), halo)
        prev = src_ref[pl.ds(prev_start, halo), :].astype(F32) * jnp.where(c > 0, 1.0, 0.0)
        nxt = src_ref[pl.ds(next_start, halo), :].astype(F32) * jnp.where(c < nc - 1, 1.0, 0.0)
        ext = jnp.concatenate([prev, cur, nxt], axis=0)
        w = w_ref[...]
        out = jnp.broadcast_to(b_ref[...], (L, dh))
        pad = CONV_WIDTH // 2
        for t in range(CONV_WIDTH):
            lo = halo + t - pad
            out = out + ext[lo:lo + L, :] * w[t:t + 1, :]
        return out * jax.nn.sigmoid(out)

    def prep(c, carry):
        start = pl.multiple_of(c * L, L)
        qc = conv_chunk(mq_ref, cwq_ref, cbq_ref, c) * (MLSTM_HEAD_DIM ** -0.5)
        q_s[pl.ds(start, L), :] = qc.astype(BF16)
        kc = conv_chunk(mk_ref, cwk_ref, cbk_ref, c)
        kt_s[c] = kc.T.astype(BF16)
        return carry

    lax.fori_loop(0, nc, prep, 0)

    g = gates_ref[...]
    b_f = _lane_scan(_log_sigmoid(g[2]), reverse=False)
    b_b = _lane_scan(_log_sigmoid(g[3]), reverse=True)
    gsc[0] = b_f
    gsc[1] = g[0] - b_f
    gsc[2] = b_b
    gsc[3] = g[1] - b_b

    c_s[...] = jnp.zeros(c_s.shape, F32)

    row_i = lax.broadcasted_iota(jnp.int32, (L, L), 0)
    col_i = lax.broadcasted_iota(jnp.int32, (L, L), 1)
    lane_row = lax.broadcasted_iota(jnp.int32, (1, L), 1)
    ones_col = jnp.where(lax.broadcasted_iota(jnp.int32, (L, LANES), 1) == 0, 1.0, 0.0).astype(BF16)

    def chunk_step(c, m, direction):
        start = pl.multiple_of(c * L, L)
        causal = (col_i <= row_i) if direction == 0 else (col_i >= row_i)
        last_lane = L - 1 if direction == 0 else 0
        q = q_s[pl.ds(start, L), :]
        kt = kt_s[c]
        vaug = jnp.concatenate([mv_ref[pl.ds(start, L), :], ones_col], axis=1)
        brow = gsc[2 * direction, pl.ds(c, 1), :]
        rrow = gsc[2 * direction + 1, pl.ds(c, 1), :]
        bcol = jnp.sum(jnp.where(row_i == col_i, brow, 0.0), axis=1, keepdims=True)
        dmat = jnp.where(causal, bcol + rrow, -jnp.inf)
        dmax = jnp.max(dmat, axis=1, keepdims=True)
        qk = jnp.dot(q, kt, preferred_element_type=F32)
        st = (qk * jnp.exp(dmat - dmax)).astype(BF16)
        p_intra = jnp.dot(st, vaug, preferred_element_type=F32)
        c_old = c_s[direction]
        p_inter = jnp.dot(q, c_old.astype(BF16), preferred_element_type=F32)
        inter = bcol + m
        m_t = jnp.maximum(inter, dmax)
        num_aug = jnp.exp(inter - m_t) * p_inter + jnp.exp(dmax - m_t) * p_intra
        den = num_aug[:, dh:dh + 1]
        hout = num_aug[:, :dh] / jnp.maximum(jnp.abs(den), jnp.exp(-m_t))
        b_last = jnp.sum(jnp.where(lane_row == last_lane, brow, 0.0), axis=1, keepdims=True)
        grow = b_last + rrow
        gmax = jnp.max(grow, axis=1, keepdims=True)
        m_new = jnp.maximum(b_last + m, gmax)
        wk = jnp.exp(grow - gmax)
        ktw = (kt.astype(F32) * wk).astype(BF16)
        upd = jnp.dot(ktw, vaug, preferred_element_type=F32)
        c_s[direction] = jnp.exp(b_last + m - m_new) * c_old + jnp.exp(gmax - m_new) * upd
        return hout, m_new

    def finalize(c, hm):
        start = pl.multiple_of(c * L, L)
        y = _rms(hm, hg_ref[...])
        og = jax.nn.sigmoid(mo_ref[pl.ds(start, L), :].astype(F32))
        o_ref[pl.ds(start, L), :] = (og * y).astype(BF16)

    def first_half(k, carry):
        m_f, m_b = carry
        c_f, c_b = k, nc - 1 - k
        h_f, m_f = chunk_step(c_f, m_f, 0)
        h_b, m_b = chunk_step(c_b, m_b, 1)
        hbuf[pl.ds(pl.multiple_of(c_f * L, L), L), :] = h_f
        hbuf[pl.ds(pl.multiple_of(c_b * L, L), L), :] = h_b
        return m_f, m_b

    def second_half(k, carry):
        m_f, m_b = carry
        c_f, c_b = k, nc - 1 - k
        h_f, m_f = chunk_step(c_f, m_f, 0)
        h_b, m_b = chunk_step(c_b, m_b, 1)
        finalize(c_f, h_f + hbuf[pl.ds(pl.multiple_of(c_f * L, L), L), :])
        finalize(c_b, h_b + hbuf[pl.ds(pl.multiple_of(c_b * L, L), L), :])
        return m_f, m_b

    m0 = jnp.zeros((1, 1), F32)
    carry = lax.fori_loop(0, nc // 2, first_half, (m0, m0), unroll=2)
    lax.fori_loop(nc // 2, nc, second_half, carry, unroll=2)


def _mlstm(proj, gates, conv_w, conv_b, head_g):
    nb, seq, _ = proj.shape
    dh = MLSTM_HEAD_DIM
    nc = seq // MLSTM_CHUNK
    assert seq % MLSTM_CHUNK == 0 and nc % 2 == 0
    base = 3 * ATT_WIDTH // dh
    nh = MLSTM_HEADS

    def col(group):
        return pl.BlockSpec((None, seq, dh), lambda b, h: (b, 0, base + group * nh + h))

    return pl.pallas_call(
        _mlstm_body,
        out_shape=jax.ShapeDtypeStruct((nb, seq, MLSTM_WIDTH), BF16),
        grid=(nb, nh),
        in_specs=[
            col(0), col(1), col(2), col(3),
            pl.BlockSpec((CONV_WIDTH, dh), lambda b, h: (0, h)),
            pl.BlockSpec((CONV_WIDTH, dh), lambda b, h: (0, nh + h)),
            pl.BlockSpec((1, dh), lambda b, h: (0, h)),
            pl.BlockSpec((1, dh), lambda b, h: (0, nh + h)),
            pl.BlockSpec((None, None, 4, nc, MLSTM_CHUNK), lambda b, h: (b, h, 0, 0, 0)),
            pl.BlockSpec((1, dh), lambda b, h: (0, h)),
        ],
        out_specs=pl.BlockSpec((None, seq, dh), lambda b, h: (b, 0, h)),
        scratch_shapes=[
            pltpu.VMEM((seq, dh), BF16),
            pltpu.VMEM((nc, dh, MLSTM_CHUNK), BF16),
            pltpu.VMEM((4, nc, MLSTM_CHUNK), F32),
            pltpu.VMEM((seq, dh), F32),
            pltpu.VMEM((2, dh, dh + LANES), F32),
        ],
        compiler_params=_params(2),
        name="mlstm",
    )(proj, proj, proj, proj, conv_w, conv_w, conv_b, conv_b, gates, head_g)


def _outproj_body(att_ref, ml_ref, wa_ref, wm_ref, x_ref, g_ref, o_ref):
    mixed = (jnp.dot(att_ref[...], wa_ref[...], preferred_element_type=F32)
             + jnp.dot(ml_ref[...], wm_ref[...], preferred_element_type=F32))
    o_ref[...] = x_ref[...] + _rms(mixed, g_ref[...])


def _out_proj(att, ml, w_out, x, g, *, tm):
    t, d = x.shape
    wa = att.shape[1]
    wm = ml.shape[1]
    assert wa == wm
    return pl.pallas_call(
        _outproj_body,
        out_shape=jax.ShapeDtypeStruct((t, d), F32),
        grid=(t // tm,),
        in_specs=[
            pl.BlockSpec((tm, wa), lambda i: (i, 0)),
            pl.BlockSpec((tm, wm), lambda i: (i, 0)),
            pl.BlockSpec((wa, d), lambda i: (0, 0)),
            pl.BlockSpec((wm, d), lambda i: (1, 0)),
            pl.BlockSpec((tm, d), lambda i: (i, 0)),
            pl.BlockSpec((1, d), lambda i: (0, 0)),
        ],
        out_specs=pl.BlockSpec((tm, d), lambda i: (i, 0)),
        compiler_params=_params(1),
        name="out_proj",
    )(att, ml, w_out, w_out, x, g)


def _layer(xs, rel_bias, ffn1_pre_g, ffn1_post_g, ffn1_w_gate, ffn1_w_up, ffn1_w_down,
           mix_pre_g, mix_post_g, w_in, gate_bias, conv_w, conv_b, lam_vecs,
           att_head_g, mlstm_head_g, w_out,
           ffn2_pre_g, ffn2_post_g, ffn2_w_gate, ffn2_w_up, ffn2_w_down,
           *, tm, tf, tn, attn_tile):
    seq, d = xs[0].shape[1:]
    rows = tuple(x.shape[0] * seq for x in xs)
    t = sum(rows)
    nb = t // seq
    row = lambda v: v.reshape(1, -1).astype(F32)

    (x1,) = _ffn(tuple(x.reshape(-1, d) for x in xs), row(ffn1_pre_g), row(ffn1_post_g),
                 ffn1_w_gate.astype(BF16), ffn1_w_up.astype(BF16), ffn1_w_down.astype(BF16), (t,),
                 tm=tm, tf=tf)

    w_main = w_in.astype(BF16)
    w_gate = jnp.pad(w_main[:, MAIN_COLS:], ((0, 0), (0, LANES - N_GATES)))
    gbias = jnp.pad(row(gate_bias), ((0, 0), (0, LANES - N_GATES)))
    col_scale = jnp.concatenate([jnp.full((1, ATT_WIDTH), ATT_QK_DIM ** -0.5 * LOG2E, F32),
                                 jnp.ones((1, MAIN_COLS - ATT_WIDTH), F32)], axis=1)
    proj, gates = _in_proj(x1, row(mix_pre_g), w_main, col_scale, w_gate, gbias, tm=tm, tn=tn)
    proj = proj.reshape(nb, seq, MAIN_COLS)
    nc = seq // MLSTM_CHUNK
    gates = gates[:, :N_GATES].reshape(nb, nc, MLSTM_CHUNK, 4, MLSTM_HEADS).transpose(0, 4, 3, 1, 2)

    btiles = _bias_tiles(rel_bias.astype(F32))
    att = _attention(proj, btiles, rel_bias.astype(F32), lam_vecs, row(att_head_g), tile=attn_tile)
    ml = _mlstm(proj, gates, conv_w.astype(F32), row(conv_b), row(mlstm_head_g))

    x2 = _out_proj(att.reshape(t, ATT_WIDTH), ml.reshape(t, MLSTM_WIDTH), w_out.astype(BF16),
                   x1, row(mix_post_g), tm=tm)
    ys = _ffn((x2,), row(ffn2_pre_g), row(ffn2_post_g), ffn2_w_gate.astype(BF16),
              ffn2_w_up.astype(BF16), ffn2_w_down.astype(BF16), rows, tm=tm, tf=tf)
    return tuple(y.reshape(x.shape) for y, x in zip(ys, xs))


def kernel(x_prompt, x_sample, rel_bias, ffn1_pre_g, ffn1_post_g, ffn1_w_gate, ffn1_w_up, ffn1_w_down, mix_pre_g, mix_post_g, w_in, gate_bias, conv_w, conv_b, lambda_q1, lambda_k1, lambda_q2, lambda_k2, att_head_g, mlstm_head_g, w_out, ffn2_pre_g, ffn2_post_g, ffn2_w_gate, ffn2_w_up, ffn2_w_down, *, tm=512, tf=512, tn=1792, attn_tile=512):
    assert x_prompt.shape[1:] == x_sample.shape[1:]
    assert ffn1_pre_g.shape[0] == 1, "single-layer trunk"
    lam_vecs = jnp.concatenate([lambda_q1, lambda_k1, lambda_q2, lambda_k2], axis=0).astype(F32)
    return _layer((x_prompt, x_sample), rel_bias, ffn1_pre_g[0], ffn1_post_g[0], ffn1_w_gate[0], ffn1_w_up[0], ffn1_w_down[0],
               mix_pre_g[0], mix_post_g[0], w_in[0], gate_bias[0], conv_w[0], conv_b[0], lam_vecs,
               att_head_g[0], mlstm_head_g[0], w_out[0],
               ffn2_pre_g[0], ffn2_post_g[0], ffn2_w_gate[0], ffn2_w_up[0], ffn2_w_down[0],
               tm=tm, tf=tf, tn=tn, attn_tile=attn_tile)
```

```python
import functools
import math

import jax
import jax.numpy as jnp
from jax import lax
from jax.experimental import pallas as pl
from jax.experimental.pallas import tpu as pltpu

F32 = jnp.float32
BF16 = jnp.bfloat16

EPS = 1e-6
ATT_HEADS = 8
ATT_QK_DIM = 64
ATT_V_DIM = 128
ATT_WIDTH = ATT_HEADS * ATT_V_DIM
MLSTM_HEADS = 4
MLSTM_HEAD_DIM = 256
MLSTM_WIDTH = MLSTM_HEADS * MLSTM_HEAD_DIM
MLSTM_CHUNK = 128
CONV_WIDTH = 5
N_GATES = 4 * MLSTM_HEADS
MAIN_COLS = 3 * ATT_WIDTH + 4 * MLSTM_WIDTH
REL_BUCKETS = 32
REL_MAX_DIST = 128
LAMBDA_INIT = 0.8 - 0.6 * math.exp(-0.3 * 0)
LOG2E = math.log2(math.e)

LANES = 128
VMEM_LIMIT = 56 * 1024 * 1024


def _params(n_axes):
    return pltpu.CompilerParams(dimension_semantics=("arbitrary",) * n_axes,
                                vmem_limit_bytes=VMEM_LIMIT)


def _rms(xf, g_row):
    ms = jnp.mean(xf * xf, axis=-1, keepdims=True)
    return xf * lax.rsqrt(ms + EPS) * g_row


def _segment_of(i, tile_starts):
    seg = 0
    for start in tile_starts[1:]:
        seg = seg + (i >= start).astype(jnp.int32)
    return seg


def _ffn_body(*refs, in_starts, out_starts):
    n_in, n_out = len(in_starts), len(out_starts)
    x_refs = refs[:n_in]
    pre_g_ref, post_g_ref, wg_ref, wu_ref, wd_ref = refs[n_in:n_in + 5]
    o_refs = refs[n_in + 5:n_in + 5 + n_out]
    h_ref, acc_ref = refs[n_in + 5 + n_out:]
    i = pl.program_id(0)
    f = pl.program_id(1)
    in_seg = _segment_of(i, in_starts)
    out_seg = _segment_of(i, out_starts)

    for k, x_ref in enumerate(x_refs):
        @pl.when(jnp.logical_and(f == 0, in_seg == k))
        def _(x_ref=x_ref):
            h_ref[...] = _rms(x_ref[...], pre_g_ref[...]).astype(BF16)
            acc_ref[...] = jnp.zeros(acc_ref.shape, F32)

    h = h_ref[...]
    g = jnp.dot(h, wg_ref[...], preferred_element_type=F32)
    u = jnp.dot(h, wu_ref[...], preferred_element_type=F32)
    a = (g * jax.nn.sigmoid(g) * u).astype(BF16)
    acc_ref[...] += jnp.dot(a, wd_ref[...], preferred_element_type=F32)

    last = f == pl.num_programs(1) - 1
    for k, x_ref in enumerate(x_refs):
        for m, o_ref in enumerate(o_refs):
            @pl.when(jnp.logical_and(last, jnp.logical_and(in_seg == k, out_seg == m)))
            def _(x_ref=x_ref, o_ref=o_ref):
                o_ref[...] = x_ref[...] + 0.5 * _rms(acc_ref[...], post_g_ref[...])


def _ffn(xs, pre_g, post_g, wg, wu, wd, out_rows, *, tm, tf):
    d = xs[0].shape[1]
    fdim = wg.shape[1]

    def tile_starts(rows):
        assert all(r % tm == 0 for r in rows)
        starts = [0]
        for r in rows[:-1]:
            starts.append(starts[-1] + r // tm)
        return tuple(starts)

    in_rows = tuple(x.shape[0] for x in xs)
    assert sum(in_rows) == sum(out_rows)
    in_starts, out_starts = tile_starts(in_rows), tile_starts(out_rows)

    def seg_spec(start, rows):
        n = rows // tm
        return pl.BlockSpec((tm, d), lambda i, f: (jnp.clip(i - start, 0, n - 1), 0))

    fixed = lambda i, f: (0, 0)
    return pl.pallas_call(
        functools.partial(_ffn_body, in_starts=in_starts, out_starts=out_starts),
        out_shape=tuple(jax.ShapeDtypeStruct((r, d), F32) for r in out_rows),
        grid=(sum(in_rows) // tm, fdim // tf),
        in_specs=[seg_spec(s, r) for s, r in zip(in_starts, in_rows)] + [
            pl.BlockSpec((1, d), fixed),
            pl.BlockSpec((1, d), fixed),
            pl.BlockSpec((d, tf), lambda i, f: (0, f)),
            pl.BlockSpec((d, tf), lambda i, f: (0, f)),
            pl.BlockSpec((tf, d), lambda i, f: (f, 0)),
        ],
        out_specs=tuple(seg_spec(s, r) for s, r in zip(out_starts, out_rows)),
        scratch_shapes=[pltpu.VMEM((tm, d), BF16), pltpu.VMEM((tm, d), F32)],
        compiler_params=_params(2),
        name="ffn",
    )(*xs, pre_g, post_g, wg, wu, wd)


def _inproj_body(x_ref, g_ref, w_ref, cs_ref, wgate_ref, gbias_ref, o_ref, gates_ref, h_ref):
    j = pl.program_id(1)

    @pl.when(j == 0)
    def _():
        h = _rms(x_ref[...], g_ref[...]).astype(BF16)
        h_ref[...] = h
        gates_ref[...] = jnp.dot(h, wgate_ref[...], preferred_element_type=F32) + gbias_ref[...]

    o_ref[...] = (jnp.dot(h_ref[...], w_ref[...], preferred_element_type=F32) * cs_ref[...]).astype(BF16)


def _in_proj(x, g, w_main, col_scale, w_gate, gate_bias, *, tm, tn):
    t, d = x.shape
    n = col_scale.shape[1]
    assert n % tn == 0 and n <= w_main.shape[1]
    return pl.pallas_call(
        _inproj_body,
        out_shape=(jax.ShapeDtypeStruct((t, n), BF16), jax.ShapeDtypeStruct((t, LANES), F32)),
        grid=(t // tm, n // tn),
        in_specs=[
            pl.BlockSpec((tm, d), lambda i, j: (i, 0)),
            pl.BlockSpec((1, d), lambda i, j: (0, 0)),
            pl.BlockSpec((d, tn), lambda i, j: (0, j)),
            pl.BlockSpec((1, tn), lambda i, j: (0, j)),
            pl.BlockSpec((d, LANES), lambda i, j: (0, 0)),
            pl.BlockSpec((1, LANES), lambda i, j: (0, 0)),
        ],
        out_specs=(pl.BlockSpec((tm, tn), lambda i, j: (i, j)),
                   pl.BlockSpec((tm, LANES), lambda i, j: (i, 0))),
        scratch_shapes=[pltpu.VMEM((tm, d), BF16)],
        compiler_params=_params(2),
        name="in_proj",
    )(x, g, w_main, col_scale, w_gate, gate_bias)


def _rel_bucket(rel):
    nb = REL_BUCKETS // 2
    max_exact = nb // 2
    ret = jnp.where(rel > 0, nb, 0)
    n = jnp.abs(rel)
    nf = jnp.maximum(n, 1).astype(jnp.float32)
    large = max_exact + (jnp.log(nf / max_exact) / math.log(REL_MAX_DIST / max_exact)
                         * (nb - max_exact)).astype(jnp.int32)
    large = jnp.minimum(large, nb - 1)
    return ret + jnp.where(n < max_exact, n, large)


def _bias_tiles_body(relb_ref, bucket_ref, o_ref):
    hc = pl.program_id(0)
    bucket = bucket_ref[...]
    acc = jnp.zeros(bucket.shape, F32)
    for b in range(REL_BUCKETS):
        acc = jnp.where(bucket == b, relb_ref[b, hc], acc)
    o_ref[...] = acc * LOG2E


def _bias_tiles(rel_bias):
    key = lax.broadcasted_iota(jnp.int32, (3, LANES, LANES), 1)
    query = lax.broadcasted_iota(jnp.int32, (3, LANES, LANES), 2)
    off = (lax.broadcasted_iota(jnp.int32, (3, LANES, LANES), 0) - 1) * LANES
    bucket = _rel_bucket(off + key - query).astype(jnp.int32)
    n_hc = rel_bias.shape[1]
    return pl.pallas_call(
        _bias_tiles_body,
        out_shape=jax.ShapeDtypeStruct((n_hc, 3, LANES, LANES), F32),
        grid=(n_hc,),
        in_specs=[
            pl.BlockSpec(memory_space=pltpu.SMEM),
            pl.BlockSpec((3, LANES, LANES), lambda i: (0, 0, 0)),
        ],
        out_specs=pl.BlockSpec((None, 3, LANES, LANES), lambda i: (i, 0, 0, 0)),
        compiler_params=_params(1),
        name="rel_bias_tiles",
    )(rel_bias, bucket)


V_ROWS = ATT_V_DIM + 16


def _attn_body(relb_ref, lam_ref, q_ref, k_ref, v_ref, btile_ref, hg_ref, o_ref,
               qz_ref, vt_ref, bias_ref, sa_ref, sb_ref, ma_ref, mb_ref, ha_ref, hb_ref, acc_ref, m_ref, *, tile):
    h = pl.program_id(1)
    seq = k_ref.shape[0]
    n_tiles = seq // tile
    n_sub = tile // LANES
    far_bucket_neg = REL_BUCKETS // 2 - 1
    far_bucket_pos = REL_BUCKETS - 1

    q = q_ref[...]
    lane = lax.broadcasted_iota(jnp.int32, q.shape, 1)
    zero = jnp.zeros_like(q)
    qz_ref[0] = jnp.where(lane < ATT_QK_DIM, q, zero)
    qz_ref[1] = jnp.where(lane >= ATT_QK_DIM, q, zero)
    ones_rows = jnp.where(lax.broadcasted_iota(jnp.int32, (V_ROWS - ATT_V_DIM, tile), 0) == 0,
                          1.0, 0.0).astype(BF16)
    for jt in range(n_tiles):
        vt_ref[jt, :ATT_V_DIM, :] = v_ref[jt * tile:(jt + 1) * tile, :].T
        vt_ref[jt, ATT_V_DIM:, :] = ones_rows
    for c in range(2):
        cneg = relb_ref[far_bucket_neg, 2 * h + c] * LOG2E
        cpos = relb_ref[far_bucket_pos, 2 * h + c] * LOG2E
        for di in range(3):
            for kj in range(n_sub):
                for qi in range(n_sub):
                    d = (di - 1) * n_sub + kj - qi
                    if -1 <= d <= 1:
                        sub = btile_ref[c, d + 1]
                    else:
                        sub = jnp.full((LANES, LANES), cneg if d < 0 else cpos, F32)
                    bias_ref[c, di, kj * LANES:(kj + 1) * LANES, qi * LANES:(qi + 1) * LANES] = sub
    m_ref[...] = jnp.full(m_ref.shape, -jnp.inf, F32)
    acc_ref[...] = jnp.zeros(acc_ref.shape, F32)

    lam_v = lam_ref[...]
    s1 = jnp.sum(lam_v[0:1] * lam_v[1:2], axis=-1, keepdims=True)
    s2 = jnp.sum(lam_v[2:3] * lam_v[3:4], axis=-1, keepdims=True)
    lam = jnp.exp(s1) - jnp.exp(s2) + LAMBDA_INIT

    def scores(qi, j, bufs, near):
        s_ref, smax_ref, shift_ref = bufs
        kt = k_ref[pl.ds(pl.multiple_of(j * tile, tile), tile), :]
        for c in range(2):
            qz = qz_ref[c, pl.ds(pl.multiple_of(qi * tile, tile), tile), :]
            s = lax.dot_general(kt, qz, (((1,), (1,)), ((), ())), preferred_element_type=F32)
            if near:
                s = s + bias_ref[c, j - qi + 1]
                shift = jnp.zeros((1, tile), F32)
            else:
                shift = jnp.full((1, tile), jnp.where(j < qi, relb_ref[far_bucket_neg, 2 * h + c],
                                                      relb_ref[far_bucket_pos, 2 * h + c]) * LOG2E, F32)
            s_ref[c] = s
            smax_ref[c] = jnp.max(s, axis=0, keepdims=True) + shift
            shift_ref[c] = shift

    def accumulate(j, bufs):
        s_ref, smax_ref, shift_ref = bufs
        vt = vt_ref[j]
        for c in range(2):
            m_old = m_ref[c]
            m_new = jnp.maximum(m_old, smax_ref[c])
            alpha = jnp.exp2(m_old - m_new)
            e = jnp.exp2(s_ref[c] - (m_new - shift_ref[c])).astype(BF16)
            acc_ref[c] = alpha * acc_ref[c] + jnp.dot(vt, e, preferred_element_type=F32)
            m_ref[c] = m_new

    def finalize(qi):
        outs = []
        for c in range(2):
            acc = acc_ref[c]
            outs.append(acc[:ATT_V_DIM] / acc[ATT_V_DIM:ATT_V_DIM + 1])
        o_t = outs[0] - lam * outs[1]
        y_t = o_t * lax.rsqrt(jnp.mean(o_t * o_t, axis=0, keepdims=True) + EPS)
        o_ref[pl.ds(pl.multiple_of(qi * tile, tile), tile), :] = (
            y_t.T * hg_ref[...] * (1.0 - LAMBDA_INIT)).astype(BF16)
        m_ref[...] = jnp.full(m_ref.shape, -jnp.inf, F32)
        acc_ref[...] = jnp.zeros(acc_ref.shape, F32)

    buf_a = (sa_ref, ma_ref, ha_ref)
    buf_b = (sb_ref, mb_ref, hb_ref)

    def is_near(qi, j):
        return jnp.abs(j - qi) <= 1

    def pair(q1, j1, q2, j2, jc1, jc2, tail=None):
        n1, n2 = is_near(q1, j1), is_near(q2, j2)
        for v1 in (True, False):
            for v2 in (True, False):
                @pl.when(jnp.logical_and(n1 == v1, n2 == v2))
                def _(v1=v1, v2=v2):
                    scores(q1, j1, buf_b, v1)
                    accumulate(jc1, buf_a)
                    scores(q2, j2, buf_a, v2)
                    accumulate(jc2, buf_b)
                    if tail is not None:
                        tail()

    scores(0, 0, buf_a, True)

    def q_body(qi, carry):
        def pair_body(t, c2):
            j = 2 * t
            pair(qi, j + 1, qi, j + 2, j, j + 1)
            return c2

        lax.fori_loop(0, n_tiles // 2 - 1, pair_body, 0)
        pair(qi, n_tiles - 1, jnp.minimum(qi + 1, n_tiles - 1), 0, n_tiles - 2, n_tiles - 1,
             tail=lambda: finalize(qi))
        return carry

    lax.fori_loop(0, n_tiles, q_body, 0)


def _attention(proj, btiles, rel_bias, lam_vecs, head_g, *, tile):
    nb, seq, _ = proj.shape
    assert seq % (2 * tile) == 0 and tile % LANES == 0
    btiles = btiles.reshape(ATT_HEADS, 2, 3, LANES, LANES)
    return pl.pallas_call(
        functools.partial(_attn_body, tile=tile),
        out_shape=jax.ShapeDtypeStruct((nb, seq, ATT_WIDTH), BF16),
        grid=(nb, ATT_HEADS),
        in_specs=[
            pl.BlockSpec(memory_space=pltpu.SMEM),
            pl.BlockSpec((4, ATT_QK_DIM), lambda b, h: (0, 0)),
            pl.BlockSpec((None, seq, LANES), lambda b, h: (b, 0, h)),
            pl.BlockSpec((None, seq, LANES), lambda b, h: (b, 0, ATT_HEADS + h)),
            pl.BlockSpec((None, seq, LANES), lambda b, h: (b, 0, 2 * ATT_HEADS + h)),
            pl.BlockSpec((None, 2, 3, LANES, LANES), lambda b, h: (h, 0, 0, 0, 0)),
            pl.BlockSpec((1, ATT_V_DIM), lambda b, h: (0, h)),
        ],
        out_specs=pl.BlockSpec((None, seq, ATT_V_DIM), lambda b, h: (b, 0, h)),
        scratch_shapes=[
            pltpu.VMEM((2, seq, LANES), BF16),
            pltpu.VMEM((seq // tile, V_ROWS, tile), BF16),
            pltpu.VMEM((2, 3, tile, tile), F32),
            pltpu.VMEM((2, tile, tile), F32),
            pltpu.VMEM((2, tile, tile), F32),
            pltpu.VMEM((2, 1, tile), F32),
            pltpu.VMEM((2, 1, tile), F32),
            pltpu.VMEM((2, 1, tile), F32),
            pltpu.VMEM((2, 1, tile), F32),
            pltpu.VMEM((2, V_ROWS, tile), F32),
            pltpu.VMEM((2, 1, tile), F32),
        ],
        compiler_params=_params(2),
        name="diff_attention",
    )(rel_bias, lam_vecs, proj, proj, proj, btiles, head_g)


def _log_sigmoid(x):
    return jnp.minimum(x, 0.0) - jnp.log1p(jnp.exp(-jnp.abs(x)))


def _lane_scan(x, reverse):
    lane = lax.broadcasted_iota(jnp.int32, x.shape, 1)
    sh = 1
    while sh < LANES:
        if reverse:
            x = x + jnp.where(lane < LANES - sh, pltpu.roll(x, LANES - sh, 1), 0.0)
        else:
            x = x + jnp.where(lane >= sh, pltpu.roll(x, sh, 1), 0.0)
        sh *= 2
    return x


def _mlstm_body(mq_ref, mk_ref, mv_ref, mo_ref, cwq_ref, cwk_ref, cbq_ref, cbk_ref, gates_ref, hg_ref,
                o_ref, q_s, kt_s, gsc, hbuf, c_s):
    seq, dh = mq_ref.shape
    L = MLSTM_CHUNK
    nc = seq // L
    halo = 16

    def conv_chunk(src_ref, w_ref, b_ref, c):
        start = pl.multiple_of(c * L, L)
        cur = src_ref[pl.ds(start, L), :].astype(F32)
        prev_start = pl.multiple_of(jnp.maximum(start - halo, 0), halo)
        next_start = pl.multiple_of(jnp.minimum(start + L, seq - halo), halo)
        prev = src_ref[pl.ds(prev_start, halo), :].astype(F32) * jnp.where(c > 0, 1.0, 0.0)
        nxt = src_ref[pl.ds(next_start, halo), :].astype(F32) * jnp.where(c < nc - 1, 1.0, 0.0)
        ext = jnp.concatenate([prev, cur, nxt], axis=0)
        w = w_ref[...]
        out = jnp.broadcast_to(b_ref[...], (L, dh))
        pad = CONV_WIDTH // 2
        for t in range(CONV_WIDTH):
            lo = halo + t - pad
            out = out + ext[lo:lo + L, :] * w[t:t + 1, :]
        return out * jax.nn.sigmoid(out)

    def prep(c, carry):
        start = pl.multiple_of(c * L, L)
        qc = conv_chunk(mq_ref, cwq_ref, cbq_ref, c) * (MLSTM_HEAD_DIM ** -0.5)
        q_s[pl.ds(start, L), :] = qc.astype(BF16)
        kc = conv_chunk(mk_ref, cwk_ref, cbk_ref, c)
        kt_s[c] = kc.T.astype(BF16)
        return carry

    lax.fori_loop(0, nc, prep, 0)

    g = gates_ref[...]
    b_f = _lane_scan(_log_sigmoid(g[2]), reverse=False)
    b_b = _lane_scan(_log_sigmoid(g[3]), reverse=True)
    gsc[0] = b_f
    gsc[1] = g[0] - b_f
    gsc[2] = b_b
    gsc[3] = g[1] - b_b

    c_s[...] = jnp.zeros(c_s.shape, F32)

    row_i = lax.broadcasted_iota(jnp.int32, (L, L), 0)
    col_i = lax.broadcasted_iota(jnp.int32, (L, L), 1)
    lane_row = lax.broadcasted_iota(jnp.int32, (1, L), 1)
    ones_col = jnp.where(lax.broadcasted_iota(jnp.int32, (L, LANES), 1) == 0, 1.0, 0.0).astype(BF16)

    def chunk_of(k, direction):
        return k if direction == 0 else nc - 1 - k

    def chunk_step(k, m, direction):
        c = chunk_of(k, direction)
        start = pl.multiple_of(c * L, L)
        causal = (col_i <= row_i) if direction == 0 else (col_i >= row_i)
        last_lane = L - 1 if direction == 0 else 0
        q = q_s[pl.ds(start, L), :]
        kt = kt_s[c]
        vaug = jnp.concatenate([mv_ref[pl.ds(start, L), :], ones_col], axis=1)
        brow = gsc[2 * direction, pl.ds(c, 1), :]
        rrow = gsc[2 * direction + 1, pl.ds(c, 1), :]
        bcol = jnp.sum(jnp.where(row_i == col_i, brow, 0.0), axis=1, keepdims=True)
        dmat = jnp.where(causal, bcol + rrow, -jnp.inf)
        dmax = jnp.max(dmat, axis=1, keepdims=True)
        c_old = c_s[direction]
        q_out = jnp.dot(q, jnp.concatenate([kt, c_old.astype(BF16)], axis=1),
                        preferred_element_type=F32)
        qk, p_inter = q_out[:, :L], q_out[:, L:]
        st = (qk * jnp.exp(dmat - dmax)).astype(BF16)
        b_last = jnp.sum(jnp.where(lane_row == last_lane, brow, 0.0), axis=1, keepdims=True)
        grow = b_last + rrow
        gmax = jnp.max(grow, axis=1, keepdims=True)
        wk = jnp.exp(grow - gmax)
        ktw = (kt.astype(F32) * wk).astype(BF16)
        v_out = jnp.dot(jnp.concatenate([st, ktw], axis=0), vaug, preferred_element_type=F32)
        p_intra, upd = v_out[:L], v_out[L:]
        inter = bcol + m
        m_t = jnp.maximum(inter, dmax)
        num_aug = jnp.exp(inter - m_t) * p_inter + jnp.exp(dmax - m_t) * p_intra
        den = num_aug[:, dh:dh + 1]
        hout = num_aug[:, :dh] / jnp.maximum(jnp.abs(den), jnp.exp(-m_t))
        m_new = jnp.maximum(b_last + m, gmax)
        c_s[direction] = jnp.exp(b_last + m - m_new) * c_old + jnp.exp(gmax - m_new) * upd
        return hout, m_new

    def finalize(c, hm):
        start = pl.multiple_of(c * L, L)
        y = _rms(hm, hg_ref[...])
        og = jax.nn.sigmoid(mo_ref[pl.ds(start, L), :].astype(F32))
        o_ref[pl.ds(start, L), :] = (og * y).astype(BF16)

    def run_half(first_step, ms, second_half):
        def body(k, ms):
            out = []
            for direction in range(2):
                hout, m_new = chunk_step(k, ms[direction], direction)
                rows = pl.ds(pl.multiple_of(chunk_of(k, direction) * L, L), L)
                if second_half:
                    finalize(chunk_of(k, direction), hout + hbuf[rows, :])
                else:
                    hbuf[rows, :] = hout
                out.append(m_new)
            return tuple(out)

        return lax.fori_loop(first_step, first_step + nc // 2, body, ms, unroll=2)

    m0 = jnp.zeros((1, 1), F32)
    ms = run_half(0, (m0, m0), False)
    run_half(nc // 2, ms, True)


def _mlstm(proj, gates, conv_w, conv_b, head_g):
    nb, seq, _ = proj.shape
    dh = MLSTM_HEAD_DIM
    nc = seq // MLSTM_CHUNK
    assert seq % MLSTM_CHUNK == 0 and nc % 2 == 0
    base = 3 * ATT_WIDTH // dh
    nh = MLSTM_HEADS

    def col(group):
        return pl.BlockSpec((None, seq, dh), lambda b, h: (b, 0, base + group * nh + h))

    return pl.pallas_call(
        _mlstm_body,
        out_shape=jax.ShapeDtypeStruct((nb, seq, MLSTM_WIDTH), BF16),
        grid=(nb, nh),
        in_specs=[
            col(0), col(1), col(2), col(3),
            pl.BlockSpec((CONV_WIDTH, dh), lambda b, h: (0, h)),
            pl.BlockSpec((CONV_WIDTH, dh), lambda b, h: (0, nh + h)),
            pl.BlockSpec((1, dh), lambda b, h: (0, h)),
            pl.BlockSpec((1, dh), lambda b, h: (0, nh + h)),
            pl.BlockSpec((None, None, 4, nc, MLSTM_CHUNK), lambda b, h: (b, h, 0, 0, 0)),
            pl.BlockSpec((1, dh), lambda b, h: (0, h)),
        ],
        out_specs=pl.BlockSpec((None, seq, dh), lambda b, h: (b, 0, h)),
        scratch_shapes=[
            pltpu.VMEM((seq, dh), BF16),
            pltpu.VMEM((nc, dh, MLSTM_CHUNK), BF16),
            pltpu.VMEM((4, nc, MLSTM_CHUNK), F32),
            pltpu.VMEM((seq, dh), F32),
            pltpu.VMEM((2, dh, dh + LANES), F32),
        ],
        compiler_params=_params(2),
        name="mlstm",
    )(proj, proj, proj, proj, conv_w, conv_w, conv_b, conv_b, gates, head_g)


def _outproj_body(att_ref, ml_ref, wa_ref, wm_ref, x_ref, g_ref, o_ref):
    mixed = (jnp.dot(att_ref[...], wa_ref[...], preferred_element_type=F32)
             + jnp.dot(ml_ref[...], wm_ref[...], preferred_element_type=F32))
    o_ref[...] = x_ref[...] + _rms(mixed, g_ref[...])


def _out_proj(att, ml, w_out, x, g, *, tm):
    t, d = x.shape
    wa = att.shape[1]
    wm = ml.shape[1]
    assert wa == wm
    return pl.pallas_call(
        _outproj_body,
        out_shape=jax.ShapeDtypeStruct((t, d), F32),
        grid=(t // tm,),
        in_specs=[
            pl.BlockSpec((tm, wa), lambda i: (i, 0)),
            pl.BlockSpec((tm, wm), lambda i: (i, 0)),
            pl.BlockSpec((wa, d), lambda i: (0, 0)),
            pl.BlockSpec((wm, d), lambda i: (1, 0)),
            pl.BlockSpec((tm, d), lambda i: (i, 0)),
            pl.BlockSpec((1, d), lambda i: (0, 0)),
        ],
        out_specs=pl.BlockSpec((tm, d), lambda i: (i, 0)),
        compiler_params=_params(1),
        name="out_proj",
    )(att, ml, w_out, w_out, x, g)


def _layer(xs, rel_bias, ffn1_pre_g, ffn1_post_g, ffn1_w_gate, ffn1_w_up, ffn1_w_down,
           mix_pre_g, mix_post_g, w_in, gate_bias, conv_w, conv_b, lam_vecs,
           att_head_g, mlstm_head_g, w_out,
           ffn2_pre_g, ffn2_post_g, ffn2_w_gate, ffn2_w_up, ffn2_w_down,
           *, tm, tf, tn, attn_tile):
    seq, d = xs[0].shape[1:]
    rows = tuple(x.shape[0] * seq for x in xs)
    t = sum(rows)
    nb = t // seq
    row = lambda v: v.reshape(1, -1).astype(F32)

    (x1,) = _ffn(tuple(x.reshape(-1, d) for x in xs), row(ffn1_pre_g), row(ffn1_post_g),
                 ffn1_w_gate.astype(BF16), ffn1_w_up.astype(BF16), ffn1_w_down.astype(BF16), (t,),
                 tm=tm, tf=tf)

    w_main = w_in.astype(BF16)
    w_gate = jnp.pad(w_main[:, MAIN_COLS:], ((0, 0), (0, LANES - N_GATES)))
    gbias = jnp.pad(row(gate_bias), ((0, 0), (0, LANES - N_GATES)))
    col_scale = jnp.concatenate([jnp.full((1, ATT_WIDTH), ATT_QK_DIM ** -0.5 * LOG2E, F32),
                                 jnp.ones((1, MAIN_COLS - ATT_WIDTH), F32)], axis=1)
    proj, gates = _in_proj(x1, row(mix_pre_g), w_main, col_scale, w_gate, gbias, tm=tm, tn=tn)
    proj = proj.reshape(nb, seq, MAIN_COLS)
    nc = seq // MLSTM_CHUNK
    gates = gates[:, :N_GATES].reshape(nb, nc, MLSTM_CHUNK, 4, MLSTM_HEADS).transpose(0, 4, 3, 1, 2)

    btiles = _bias_tiles(rel_bias.astype(F32))
    att = _attention(proj, btiles, rel_bias.astype(F32), lam_vecs, row(att_head_g), tile=attn_tile)
    ml = _mlstm(proj, gates, conv_w.astype(F32), row(conv_b), row(mlstm_head_g))

    x2 = _out_proj(att.reshape(t, ATT_WIDTH), ml.reshape(t, MLSTM_WIDTH), w_out.astype(BF16),
                   x1, row(mix_post_g), tm=tm)
    ys = _ffn((x2,), row(ffn2_pre_g), row(ffn2_post_g), ffn2_w_gate.astype(BF16),
              ffn2_w_up.astype(BF16), ffn2_w_down.astype(BF16), rows, tm=tm, tf=tf)
    return tuple(y.reshape(x.shape) for y, x in zip(ys, xs))


def kernel(x_prompt, x_sample, rel_bias, ffn1_pre_g, ffn1_post_g, ffn1_w_gate, ffn1_w_up, ffn1_w_down, mix_pre_g, mix_post_g, w_in, gate_bias, conv_w, conv_b, lambda_q1, lambda_k1, lambda_q2, lambda_k2, att_head_g, mlstm_head_g, w_out, ffn2_pre_g, ffn2_post_g, ffn2_w_gate, ffn2_w_up, ffn2_w_down, *, tm=512, tf=512, tn=1792, attn_tile=512):
    assert x_prompt.shape[1:] == x_sample.shape[1:]
    assert ffn1_pre_g.shape[0] == 1, "single-layer trunk"
    lam_vecs = jnp.concatenate([lambda_q1, lambda_k1, lambda_q2, lambda_k2], axis=0).astype(F32)
    return _layer((x_prompt, x_sample), rel_bias, ffn1_pre_g[0], ffn1_post_g[0], ffn1_w_gate[0], ffn1_w_up[0], ffn1_w_down[0],
               mix_pre_g[0], mix_post_g[0], w_in[0], gate_bias[0], conv_w[0], conv_b[0], lam_vecs,
               att_head_g[0], mlstm_head_g[0], w_out[0],
               ffn2_pre_g[0], ffn2_post_g[0], ffn2_w_gate[0], ffn2_w_up[0], ffn2_w_down[0],
               tm=tm, tf=tf, tn=tn, attn_tile=attn_tile)
```

```python
import functools
import math

import jax
import jax.numpy as jnp
from jax import lax
from jax.experimental import pallas as pl
from jax.experimental.pallas import tpu as pltpu

F32 = jnp.float32
BF16 = jnp.bfloat16

EPS = 1e-6
ATT_HEADS = 8
ATT_QK_DIM = 64
ATT_V_DIM = 128
ATT_WIDTH = ATT_HEADS * ATT_V_DIM
MLSTM_HEADS = 4
MLSTM_HEAD_DIM = 256
MLSTM_WIDTH = MLSTM_HEADS * MLSTM_HEAD_DIM
MLSTM_CHUNK = 128
CONV_WIDTH = 5
N_GATES = 4 * MLSTM_HEADS
MAIN_COLS = 3 * ATT_WIDTH + 4 * MLSTM_WIDTH
REL_BUCKETS = 32
REL_MAX_DIST = 128
LAMBDA_INIT = 0.8 - 0.6 * math.exp(-0.3 * 0)
LOG2E = math.log2(math.e)

LANES = 128
VMEM_LIMIT = 56 * 1024 * 1024


def _params(n_axes):
    return pltpu.CompilerParams(dimension_semantics=("arbitrary",) * n_axes,
                                vmem_limit_bytes=VMEM_LIMIT)


def _rms(xf, g_row):
    ms = jnp.mean(xf * xf, axis=-1, keepdims=True)
    return xf * lax.rsqrt(ms + EPS) * g_row


def _segment_of(i, tile_starts):
    seg = 0
    for start in tile_starts[1:]:
        seg = seg + (i >= start).astype(jnp.int32)
    return seg


def _ffn_body(*refs, in_starts, out_starts):
    n_in, n_out = len(in_starts), len(out_starts)
    x_refs = refs[:n_in]
    pre_g_ref, post_g_ref, wg_ref, wu_ref, wd_ref = refs[n_in:n_in + 5]
    o_refs = refs[n_in + 5:n_in + 5 + n_out]
    h_ref, acc_ref = refs[n_in + 5 + n_out:]
    i = pl.program_id(0)
    f = pl.program_id(1)
    in_seg = _segment_of(i, in_starts)
    out_seg = _segment_of(i, out_starts)

    for k, x_ref in enumerate(x_refs):
        @pl.when(jnp.logical_and(f == 0, in_seg == k))
        def _(x_ref=x_ref):
            h_ref[...] = _rms(x_ref[...], pre_g_ref[...]).astype(BF16)
            acc_ref[...] = jnp.zeros(acc_ref.shape, F32)

    h = h_ref[...]
    g = jnp.dot(h, wg_ref[...], preferred_element_type=F32)
    u = jnp.dot(h, wu_ref[...], preferred_element_type=F32)
    a = (g * jax.nn.sigmoid(g) * u).astype(BF16)
    acc_ref[...] += jnp.dot(a, wd_ref[...], preferred_element_type=F32)

    last = f == pl.num_programs(1) - 1
    for k, x_ref in enumerate(x_refs):
        for m, o_ref in enumerate(o_refs):
            @pl.when(jnp.logical_and(last, jnp.logical_and(in_seg == k, out_seg == m)))
            def _(x_ref=x_ref, o_ref=o_ref):
                o_ref[...] = x_ref[...] + _rms(acc_ref[...], 0.5 * post_g_ref[...])


def _ffn(xs, pre_g, post_g, wg, wu, wd, out_rows, *, tm, tf):
    d = xs[0].shape[1]
    fdim = wg.shape[1]

    def tile_starts(rows):
        assert all(r % tm == 0 for r in rows)
        starts = [0]
        for r in rows[:-1]:
            starts.append(starts[-1] + r // tm)
        return tuple(starts)

    in_rows = tuple(x.shape[0] for x in xs)
    assert sum(in_rows) == sum(out_rows)
    in_starts, out_starts = tile_starts(in_rows), tile_starts(out_rows)

    def seg_spec(start, rows):
        n = rows // tm
        return pl.BlockSpec((tm, d), lambda i, f: (jnp.clip(i - start, 0, n - 1), 0))

    fixed = lambda i, f: (0, 0)
    return pl.pallas_call(
        functools.partial(_ffn_body, in_starts=in_starts, out_starts=out_starts),
        out_shape=tuple(jax.ShapeDtypeStruct((r, d), F32) for r in out_rows),
        grid=(sum(in_rows) // tm, fdim // tf),
        in_specs=[seg_spec(s, r) for s, r in zip(in_starts, in_rows)] + [
            pl.BlockSpec((1, d), fixed),
            pl.BlockSpec((1, d), fixed),
            pl.BlockSpec((d, tf), lambda i, f: (0, f)),
            pl.BlockSpec((d, tf), lambda i, f: (0, f)),
            pl.BlockSpec((tf, d), lambda i, f: (f, 0)),
        ],
        out_specs=tuple(seg_spec(s, r) for s, r in zip(out_starts, out_rows)),
        scratch_shapes=[pltpu.VMEM((tm, d), BF16), pltpu.VMEM((tm, d), F32)],
        compiler_params=_params(2),
        name="ffn",
    )(*xs, pre_g, post_g, wg, wu, wd)


def _inproj_body(x_ref, g_ref, w_ref, cs_ref, wgate_ref, gbias_ref, o_ref, gates_ref, h_ref):
    j = pl.program_id(1)

    @pl.when(j == 0)
    def _():
        h = _rms(x_ref[...], g_ref[...]).astype(BF16)
        h_ref[...] = h
        gates = jnp.dot(h, wgate_ref[...], preferred_element_type=F32) + gbias_ref[...]
        gates_ref[...] = gates.T

    o_ref[...] = (jnp.dot(h_ref[...], w_ref[...], preferred_element_type=F32) * cs_ref[...]).astype(BF16)


def _in_proj(x, g, w_main, col_scale, w_gate, gate_bias, *, tm, tn):
    t, d = x.shape
    n = col_scale.shape[1]
    assert n % tn == 0 and n <= w_main.shape[1]
    return pl.pallas_call(
        _inproj_body,
        out_shape=(jax.ShapeDtypeStruct((t, n), BF16), jax.ShapeDtypeStruct((LANES, t), F32)),
        grid=(t // tm, n // tn),
        in_specs=[
            pl.BlockSpec((tm, d), lambda i, j: (i, 0)),
            pl.BlockSpec((1, d), lambda i, j: (0, 0)),
            pl.BlockSpec((d, tn), lambda i, j: (0, j)),
            pl.BlockSpec((1, tn), lambda i, j: (0, j)),
            pl.BlockSpec((d, LANES), lambda i, j: (0, 0)),
            pl.BlockSpec((1, LANES), lambda i, j: (0, 0)),
        ],
        out_specs=(pl.BlockSpec((tm, tn), lambda i, j: (i, j)),
                   pl.BlockSpec((LANES, tm), lambda i, j: (0, i))),
        scratch_shapes=[pltpu.VMEM((tm, d), BF16)],
        compiler_params=_params(2),
        name="in_proj",
    )(x, g, w_main, col_scale, w_gate, gate_bias)


def _rel_bucket(rel):
    nb = REL_BUCKETS // 2
    max_exact = nb // 2
    ret = jnp.where(rel > 0, nb, 0)
    n = jnp.abs(rel)
    nf = jnp.maximum(n, 1).astype(jnp.float32)
    large = max_exact + (jnp.log(nf / max_exact) / math.log(REL_MAX_DIST / max_exact)
                         * (nb - max_exact)).astype(jnp.int32)
    large = jnp.minimum(large, nb - 1)
    return ret + jnp.where(n < max_exact, n, large)


def _bias_tiles_body(relb_ref, bucket_ref, o_ref):
    hc = pl.program_id(0)
    bucket = bucket_ref[...]
    acc = jnp.zeros(bucket.shape, F32)
    for b in range(REL_BUCKETS):
        acc = jnp.where(bucket == b, relb_ref[b, hc], acc)
    o_ref[...] = acc * LOG2E


def _bias_tiles(rel_bias):
    key = lax.broadcasted_iota(jnp.int32, (3, LANES, LANES), 1)
    query = lax.broadcasted_iota(jnp.int32, (3, LANES, LANES), 2)
    off = (lax.broadcasted_iota(jnp.int32, (3, LANES, LANES), 0) - 1) * LANES
    bucket = _rel_bucket(off + key - query).astype(jnp.int32)
    n_hc = rel_bias.shape[1]
    return pl.pallas_call(
        _bias_tiles_body,
        out_shape=jax.ShapeDtypeStruct((n_hc, 3, LANES, LANES), F32),
        grid=(n_hc,),
        in_specs=[
            pl.BlockSpec(memory_space=pltpu.SMEM),
            pl.BlockSpec((3, LANES, LANES), lambda i: (0, 0, 0)),
        ],
        out_specs=pl.BlockSpec((None, 3, LANES, LANES), lambda i: (i, 0, 0, 0)),
        compiler_params=_params(1),
        name="rel_bias_tiles",
    )(rel_bias, bucket)


V_ROWS = ATT_V_DIM + 16


def _attn_body(relb_ref, lam_ref, q_ref, k_ref, v_ref, btile_ref, hg_ref, o_ref,
               qz_ref, vt_ref, bias_ref, sa_ref, sb_ref, ma_ref, mb_ref, ha_ref, hb_ref, acc_ref, m_ref, *, tile):
    h = pl.program_id(0)
    seq = k_ref.shape[0]
    n_tiles = seq // tile
    n_sub = tile // LANES
    far_bucket_neg = REL_BUCKETS // 2 - 1
    far_bucket_pos = REL_BUCKETS - 1

    q = q_ref[...]
    lane = lax.broadcasted_iota(jnp.int32, q.shape, 1)
    zero = jnp.zeros_like(q)
    qz_ref[0] = jnp.where(lane < ATT_QK_DIM, q, zero)
    qz_ref[1] = jnp.where(lane >= ATT_QK_DIM, q, zero)
    ones_rows = jnp.where(lax.broadcasted_iota(jnp.int32, (V_ROWS - ATT_V_DIM, tile), 0) == 0,
                          1.0, 0.0).astype(BF16)
    for jt in range(n_tiles):
        vt_ref[jt, :ATT_V_DIM, :] = v_ref[jt * tile:(jt + 1) * tile, :].T
        vt_ref[jt, ATT_V_DIM:, :] = ones_rows
    @pl.when(pl.program_id(1) == 0)
    def _():
        for c in range(2):
            cneg = relb_ref[far_bucket_neg, 2 * h + c] * LOG2E
            cpos = relb_ref[far_bucket_pos, 2 * h + c] * LOG2E
            for di in range(3):
                for kj in range(n_sub):
                    for qi in range(n_sub):
                        d = (di - 1) * n_sub + kj - qi
                        if -1 <= d <= 1:
                            sub = btile_ref[c, d + 1]
                        else:
                            sub = jnp.full((LANES, LANES), cneg if d < 0 else cpos, F32)
                        bias_ref[c, di, kj * LANES:(kj + 1) * LANES, qi * LANES:(qi + 1) * LANES] = sub

    m_ref[...] = jnp.full(m_ref.shape, -jnp.inf, F32)
    acc_ref[...] = jnp.zeros(acc_ref.shape, F32)

    lam_v = lam_ref[...]
    s1 = jnp.sum(lam_v[0:1] * lam_v[1:2], axis=-1, keepdims=True)
    s2 = jnp.sum(lam_v[2:3] * lam_v[3:4], axis=-1, keepdims=True)
    lam = jnp.exp(s1) - jnp.exp(s2) + LAMBDA_INIT

    def scores(qi, j, bufs, near):
        s_ref, smax_ref, shift_ref = bufs
        kt = k_ref[pl.ds(pl.multiple_of(j * tile, tile), tile), :]
        for c in range(2):
            qz = qz_ref[c, pl.ds(pl.multiple_of(qi * tile, tile), tile), :]
            s = lax.dot_general(kt, qz, (((1,), (1,)), ((), ())), preferred_element_type=F32)
            if near:
                s = s + bias_ref[c, j - qi + 1]
                shift = jnp.zeros((1, tile), F32)
            else:
                shift = jnp.full((1, tile), jnp.where(j < qi, relb_ref[far_bucket_neg, 2 * h + c],
                                                      relb_ref[far_bucket_pos, 2 * h + c]) * LOG2E, F32)
            s_ref[c] = s
            smax_ref[c] = jnp.max(s, axis=0, keepdims=True) + shift
            shift_ref[c] = shift

    def accumulate(j, bufs):
        s_ref, smax_ref, shift_ref = bufs
        vt = vt_ref[j]
        for c in range(2):
            m_old = m_ref[c]
            m_new = jnp.maximum(m_old, smax_ref[c])
            alpha = jnp.exp2(m_old - m_new)
            e = jnp.exp2(s_ref[c] - (m_new - shift_ref[c])).astype(BF16)
            acc_ref[c] = alpha * acc_ref[c] + jnp.dot(vt, e, preferred_element_type=F32)
            m_ref[c] = m_new

    def finalize(qi):
        outs = []
        for c in range(2):
            acc = acc_ref[c]
            outs.append(acc[:ATT_V_DIM] / acc[ATT_V_DIM:ATT_V_DIM + 1])
        o_t = outs[0] - lam * outs[1]
        y_t = o_t * lax.rsqrt(jnp.mean(o_t * o_t, axis=0, keepdims=True) + EPS)
        o_ref[pl.ds(pl.multiple_of(qi * tile, tile), tile), :] = (
            y_t.T * hg_ref[...] * (1.0 - LAMBDA_INIT)).astype(BF16)
        m_ref[...] = jnp.full(m_ref.shape, -jnp.inf, F32)
        acc_ref[...] = jnp.zeros(acc_ref.shape, F32)

    buf_a = (sa_ref, ma_ref, ha_ref)
    buf_b = (sb_ref, mb_ref, hb_ref)

    def is_near(qi, j):
        return jnp.abs(j - qi) <= 1

    def pair(q1, j1, q2, j2, jc1, jc2, tail=None):
        n1, n2 = is_near(q1, j1), is_near(q2, j2)
        for v1 in (True, False):
            for v2 in (True, False):
                @pl.when(jnp.logical_and(n1 == v1, n2 == v2))
                def _(v1=v1, v2=v2):
                    scores(q1, j1, buf_b, v1)
                    accumulate(jc1, buf_a)
                    scores(q2, j2, buf_a, v2)
                    accumulate(jc2, buf_b)
                    if tail is not None:
                        tail()

    scores(0, 0, buf_a, True)

    def q_body(qi, carry):
        def pair_body(t, c2):
            j = 2 * t
            pair(qi, j + 1, qi, j + 2, j, j + 1)
            return c2

        lax.fori_loop(0, n_tiles // 2 - 1, pair_body, 0)
        pair(qi, n_tiles - 1, jnp.minimum(qi + 1, n_tiles - 1), 0, n_tiles - 2, n_tiles - 1,
             tail=lambda: finalize(qi))
        return carry

    lax.fori_loop(0, n_tiles, q_body, 0)


def _attention(proj, btiles, rel_bias, lam_vecs, head_g, *, tile):
    nb, seq, _ = proj.shape
    assert seq % (2 * tile) == 0 and tile % LANES == 0
    btiles = btiles.reshape(ATT_HEADS, 2, 3, LANES, LANES)
    return pl.pallas_call(
        functools.partial(_attn_body, tile=tile),
        out_shape=jax.ShapeDtypeStruct((nb, seq, ATT_WIDTH), BF16),
        grid=(ATT_HEADS, nb),
        in_specs=[
            pl.BlockSpec(memory_space=pltpu.SMEM),
            pl.BlockSpec((4, ATT_QK_DIM), lambda h, b: (0, 0)),
            pl.BlockSpec((None, seq, LANES), lambda h, b: (b, 0, h)),
            pl.BlockSpec((None, seq, LANES), lambda h, b: (b, 0, ATT_HEADS + h)),
            pl.BlockSpec((None, seq, LANES), lambda h, b: (b, 0, 2 * ATT_HEADS + h)),
            pl.BlockSpec((None, 2, 3, LANES, LANES), lambda h, b: (h, 0, 0, 0, 0)),
            pl.BlockSpec((1, ATT_V_DIM), lambda h, b: (0, h)),
        ],
        out_specs=pl.BlockSpec((None, seq, ATT_V_DIM), lambda h, b: (b, 0, h)),
        scratch_shapes=[
            pltpu.VMEM((2, seq, LANES), BF16),
            pltpu.VMEM((seq // tile, V_ROWS, tile), BF16),
            pltpu.VMEM((2, 3, tile, tile), F32),
            pltpu.VMEM((2, tile, tile), F32),
            pltpu.VMEM((2, tile, tile), F32),
            pltpu.VMEM((2, 1, tile), F32),
            pltpu.VMEM((2, 1, tile), F32),
            pltpu.VMEM((2, 1, tile), F32),
            pltpu.VMEM((2, 1, tile), F32),
            pltpu.VMEM((2, V_ROWS, tile), F32),
            pltpu.VMEM((2, 1, tile), F32),
        ],
        compiler_params=_params(2),
        name="diff_attention",
    )(rel_bias, lam_vecs, proj, proj, proj, btiles, head_g)


def _log_sigmoid(x):
    return jnp.minimum(x, 0.0) - jnp.log1p(jnp.exp(-jnp.abs(x)))


def _lane_scan(x, reverse):
    lane = lax.broadcasted_iota(jnp.int32, x.shape, 1)
    sh = 1
    while sh < LANES:
        if reverse:
            x = x + jnp.where(lane < LANES - sh, pltpu.roll(x, LANES - sh, 1), 0.0)
        else:
            x = x + jnp.where(lane >= sh, pltpu.roll(x, sh, 1), 0.0)
        sh *= 2
    return x


def _mlstm_body(mq_ref, mk_ref, mv_ref, mo_ref, cwq_ref, cwk_ref, cbq_ref, cbk_ref, gates_ref, hg_ref,
                o_ref, q_s, kt_s, gsc, hbuf, c_s):
    seq, dh = mq_ref.shape
    L = MLSTM_CHUNK
    nc = seq // L
    halo = 16

    def conv_chunk(src_ref, w_ref, b_ref, c):
        start = pl.multiple_of(c * L, L)
        cur = src_ref[pl.ds(start, L), :].astype(F32)
        prev_start = pl.multiple_of(jnp.maximum(start - halo, 0), halo)
        next_start = pl.multiple_of(jnp.minimum(start + L, seq - halo), halo)
        prev = src_ref[pl.ds(prev_start, halo), :].astype(F32) * jnp.where(c > 0, 1.0, 0.0)
        nxt = src_ref[pl.ds(next_start, halo), :].astype(F32) * jnp.where(c < nc - 1, 1.0, 0.0)
        ext = jnp.concatenate([prev, cur, nxt], axis=0)
        w = w_ref[...]
        out = jnp.broadcast_to(b_ref[...], (L, dh))
        pad = CONV_WIDTH // 2
        for t in range(CONV_WIDTH):
            lo = halo + t - pad
            out = out + ext[lo:lo + L, :] * w[t:t + 1, :]
        return out * jax.nn.sigmoid(out)

    def prep(c, carry):
        start = pl.multiple_of(c * L, L)
        qc = conv_chunk(mq_ref, cwq_ref, cbq_ref, c) * (MLSTM_HEAD_DIM ** -0.5)
        q_s[pl.ds(start, L), :] = qc.astype(BF16)
        kc = conv_chunk(mk_ref, cwk_ref, cbk_ref, c)
        kt_s[c] = kc.T.astype(BF16)
        return carry

    lax.fori_loop(0, nc, prep, 0)

    g = gates_ref[...]
    b_f = _lane_scan(_log_sigmoid(g[2]), reverse=False)
    b_b = _lane_scan(_log_sigmoid(g[3]), reverse=True)
    gsc[0] = b_f
    gsc[1] = g[0] - b_f
    gsc[2] = b_b
    gsc[3] = g[1] - b_b

    c_s[...] = jnp.zeros(c_s.shape, F32)

    row_i = lax.broadcasted_iota(jnp.int32, (L, L), 0)
    col_i = lax.broadcasted_iota(jnp.int32, (L, L), 1)
    lane_row = lax.broadcasted_iota(jnp.int32, (1, L), 1)
    ones_col = jnp.where(lax.broadcasted_iota(jnp.int32, (L, LANES), 1) == 0, 1.0, 0.0).astype(BF16)

    def chunk_of(k, direction):
        return k if direction == 0 else nc - 1 - k

    def chunk_step(k, m, direction):
        c = chunk_of(k, direction)
        start = pl.multiple_of(c * L, L)
        causal = (col_i <= row_i) if direction == 0 else (col_i >= row_i)
        last_lane = L - 1 if direction == 0 else 0
        q = q_s[pl.ds(start, L), :]
        kt = kt_s[c]
        vaug = jnp.concatenate([mv_ref[pl.ds(start, L), :], ones_col], axis=1)
        brow = gsc[2 * direction, pl.ds(c, 1), :]
        rrow = gsc[2 * direction + 1, pl.ds(c, 1), :]
        bcol = jnp.sum(jnp.where(row_i == col_i, brow, 0.0), axis=1, keepdims=True)
        dmat = jnp.where(causal, bcol + rrow, -jnp.inf)
        dmax = jnp.max(dmat, axis=1, keepdims=True)
        c_old = c_s[direction]
        q_out = jnp.dot(q, jnp.concatenate([kt, c_old.astype(BF16)], axis=1),
                        preferred_element_type=F32)
        qk, p_inter = q_out[:, :L], q_out[:, L:]
        st = (qk * jnp.exp(dmat - dmax)).astype(BF16)
        b_last = jnp.sum(jnp.where(lane_row == last_lane, brow, 0.0), axis=1, keepdims=True)
        grow = b_last + rrow
        gmax = jnp.max(grow, axis=1, keepdims=True)
        wk = jnp.exp(grow - gmax)
        ktw = (kt.astype(F32) * wk).astype(BF16)
        v_out = jnp.dot(jnp.concatenate([st, ktw], axis=0), vaug, preferred_element_type=F32)
        p_intra, upd = v_out[:L], v_out[L:]
        inter = bcol + m
        m_t = jnp.maximum(inter, dmax)
        num_aug = jnp.exp(inter - m_t) * p_inter + jnp.exp(dmax - m_t) * p_intra
        den = num_aug[:, dh:dh + 1]
        hout = num_aug[:, :dh] / jnp.maximum(jnp.abs(den), jnp.exp(-m_t))
        m_new = jnp.maximum(b_last + m, gmax)
        c_s[direction] = jnp.exp(b_last + m - m_new) * c_old + jnp.exp(gmax - m_new) * upd
        return hout, m_new

    def finalize(c, hm):
        start = pl.multiple_of(c * L, L)
        y = _rms(hm, hg_ref[...])
        og = jax.nn.sigmoid(mo_ref[pl.ds(start, L), :].astype(F32))
        o_ref[pl.ds(start, L), :] = (og * y).astype(BF16)

    def run_half(first_step, ms, second_half):
        def body(k, ms):
            out = []
            for direction in range(2):
                hout, m_new = chunk_step(k, ms[direction], direction)
                rows = pl.ds(pl.multiple_of(chunk_of(k, direction) * L, L), L)
                if second_half:
                    finalize(chunk_of(k, direction), hout + hbuf[rows, :])
                else:
                    hbuf[rows, :] = hout
                out.append(m_new)
            return tuple(out)

        return lax.fori_loop(first_step, first_step + nc // 2, body, ms, unroll=2)

    m0 = jnp.zeros((1, 1), F32)
    ms = run_half(0, (m0, m0), False)
    run_half(nc // 2, ms, True)


def _mlstm(proj, gates, conv_w, conv_b, head_g):
    nb, seq, _ = proj.shape
    dh = MLSTM_HEAD_DIM
    nc = seq // MLSTM_CHUNK
    assert seq % MLSTM_CHUNK == 0 and nc % 2 == 0
    base = 3 * ATT_WIDTH // dh
    nh = MLSTM_HEADS

    def col(group):
        return pl.BlockSpec((None, seq, dh), lambda b, h: (b, 0, base + group * nh + h))

    return pl.pallas_call(
        _mlstm_body,
        out_shape=jax.ShapeDtypeStruct((nb, seq, MLSTM_WIDTH), BF16),
        grid=(nb, nh),
        in_specs=[
            col(0), col(1), col(2), col(3),
            pl.BlockSpec((CONV_WIDTH, dh), lambda b, h: (0, h)),
            pl.BlockSpec((CONV_WIDTH, dh), lambda b, h: (0, nh + h)),
            pl.BlockSpec((1, dh), lambda b, h: (0, h)),
            pl.BlockSpec((1, dh), lambda b, h: (0, nh + h)),
            pl.BlockSpec((4, None, None, nc, MLSTM_CHUNK), lambda b, h: (0, h, b, 0, 0)),
            pl.BlockSpec((1, dh), lambda b, h: (0, h)),
        ],
        out_specs=pl.BlockSpec((None, seq, dh), lambda b, h: (b, 0, h)),
        scratch_shapes=[
            pltpu.VMEM((seq, dh), BF16),
            pltpu.VMEM((nc, dh, MLSTM_CHUNK), BF16),
            pltpu.VMEM((4, nc, MLSTM_CHUNK), F32),
            pltpu.VMEM((seq, dh), F32),
            pltpu.VMEM((2, dh, dh + LANES), F32),
        ],
        compiler_params=_params(2),
        name="mlstm",
    )(proj, proj, proj, proj, conv_w, conv_w, conv_b, conv_b, gates, head_g)


def _outproj_body(att_ref, ml_ref, wa_ref, wm_ref, x_ref, g_ref, o_ref):
    mixed = (jnp.dot(att_ref[...], wa_ref[...], preferred_element_type=F32)
             + jnp.dot(ml_ref[...], wm_ref[...], preferred_element_type=F32))
    o_ref[...] = x_ref[...] + _rms(mixed, g_ref[...])


def _out_proj(att, ml, w_out, x, g, *, tm):
    t, d = x.shape
    wa = att.shape[1]
    wm = ml.shape[1]
    assert wa == wm
    return pl.pallas_call(
        _outproj_body,
        out_shape=jax.ShapeDtypeStruct((t, d), F32),
        grid=(t // tm,),
        in_specs=[
            pl.BlockSpec((tm, wa), lambda i: (i, 0)),
            pl.BlockSpec((tm, wm), lambda i: (i, 0)),
            pl.BlockSpec((wa, d), lambda i: (0, 0)),
            pl.BlockSpec((wm, d), lambda i: (1, 0)),
            pl.BlockSpec((tm, d), lambda i: (i, 0)),
            pl.BlockSpec((1, d), lambda i: (0, 0)),
        ],
        out_specs=pl.BlockSpec((tm, d), lambda i: (i, 0)),
        compiler_params=_params(1),
        name="out_proj",
    )(att, ml, w_out, w_out, x, g)


def _layer(xs, rel_bias, ffn1_pre_g, ffn1_post_g, ffn1_w_gate, ffn1_w_up, ffn1_w_down,
           mix_pre_g, mix_post_g, w_in, gate_bias, conv_w, conv_b, lam_vecs,
           att_head_g, mlstm_head_g, w_out,
           ffn2_pre_g, ffn2_post_g, ffn2_w_gate, ffn2_w_up, ffn2_w_down,
           *, tm, tm_proj, tf, tn, attn_tile):
    seq, d = xs[0].shape[1:]
    rows = tuple(x.shape[0] * seq for x in xs)
    t = sum(rows)
    nb = t // seq
    row = lambda v: v.reshape(1, -1).astype(F32)

    (x1,) = _ffn(tuple(x.reshape(-1, d) for x in xs), row(ffn1_pre_g), row(ffn1_post_g),
                 ffn1_w_gate.astype(BF16), ffn1_w_up.astype(BF16), ffn1_w_down.astype(BF16), (t,),
                 tm=tm, tf=tf)

    w_main = w_in.astype(BF16)
    w_gate = jnp.pad(w_main[:, MAIN_COLS:], ((0, 0), (0, LANES - N_GATES)))
    gbias = jnp.pad(row(gate_bias), ((0, 0), (0, LANES - N_GATES)))
    col_scale = jnp.concatenate([jnp.full((1, ATT_WIDTH), ATT_QK_DIM ** -0.5 * LOG2E, F32),
                                 jnp.ones((1, MAIN_COLS - ATT_WIDTH), F32)], axis=1)
    proj, gates = _in_proj(x1, row(mix_pre_g), w_main, col_scale, w_gate, gbias, tm=tm_proj, tn=tn)
    proj = proj.reshape(nb, seq, MAIN_COLS)
    nc = seq // MLSTM_CHUNK
    gates = gates.reshape(LANES // MLSTM_HEADS, MLSTM_HEADS, nb, nc, MLSTM_CHUNK)

    btiles = _bias_tiles(rel_bias.astype(F32))
    att = _attention(proj, btiles, rel_bias.astype(F32), lam_vecs, row(att_head_g), tile=attn_tile)
    ml = _mlstm(proj, gates, conv_w.astype(F32), row(conv_b), row(mlstm_head_g))

    x2 = _out_proj(att.reshape(t, ATT_WIDTH), ml.reshape(t, MLSTM_WIDTH), w_out.astype(BF16),
                   x1, row(mix_post_g), tm=tm)
    ys = _ffn((x2,), row(ffn2_pre_g), row(ffn2_post_g), ffn2_w_gate.astype(BF16),
              ffn2_w_up.astype(BF16), ffn2_w_down.astype(BF16), rows, tm=tm, tf=tf)
    return tuple(y.reshape(x.shape) for y, x in zip(ys, xs))


def kernel(x_prompt, x_sample, rel_bias, ffn1_pre_g, ffn1_post_g, ffn1_w_gate, ffn1_w_up, ffn1_w_down, mix_pre_g, mix_post_g, w_in, gate_bias, conv_w, conv_b, lambda_q1, lambda_k1, lambda_q2, lambda_k2, att_head_g, mlstm_head_g, w_out, ffn2_pre_g, ffn2_post_g, ffn2_w_gate, ffn2_w_up, ffn2_w_down, *, tm=512, tm_proj=1024, tf=512, tn=1792, attn_tile=512):
    assert x_prompt.shape[1:] == x_sample.shape[1:]
    assert ffn1_pre_g.shape[0] == 1, "single-layer trunk"
    lam_vecs = jnp.concatenate([lambda_q1, lambda_k1, lambda_q2, lambda_k2], axis=0).astype(F32)
    return _layer((x_prompt, x_sample), rel_bias, ffn1_pre_g[0], ffn1_post_g[0], ffn1_w_gate[0], ffn1_w_up[0], ffn1_w_down[0],
               mix_pre_g[0], mix_post_g[0], w_in[0], gate_bias[0], conv_w[0], conv_b[0], lam_vecs,
               att_head_g[0], mlstm_head_g[0], w_out[0],
               ffn2_pre_g[0], ffn2_post_g[0], ffn2_w_gate[0], ffn2_w_up[0], ffn2_w_down[0],
               tm=tm, tm_proj=tm_proj, tf=tf, tn=tn, attn_tile=attn_tile)
```

```python
import functools
import math

import jax
import jax.numpy as jnp
from jax import lax
from jax.experimental import pallas as pl
from jax.experimental.pallas import tpu as pltpu

F32 = jnp.float32
BF16 = jnp.bfloat16

EPS = 1e-6
ATT_HEADS = 8
ATT_QK_DIM = 64
ATT_V_DIM = 128
ATT_WIDTH = ATT_HEADS * ATT_V_DIM
MLSTM_HEADS = 4
MLSTM_HEAD_DIM = 256
MLSTM_WIDTH = MLSTM_HEADS * MLSTM_HEAD_DIM
MLSTM_CHUNK = 128
CONV_WIDTH = 5
N_GATES = 4 * MLSTM_HEADS
MAIN_COLS = 3 * ATT_WIDTH + 4 * MLSTM_WIDTH
REL_BUCKETS = 32
REL_MAX_DIST = 128
LAMBDA_INIT = 0.8 - 0.6 * math.exp(-0.3 * 0)
LOG2E = math.log2(math.e)

LANES = 128
BF16_ROWS = 16
VMEM_LIMIT = 56 * 1024 * 1024
assert REL_MAX_DIST <= LANES


def _params(n_axes):
    return pltpu.CompilerParams(dimension_semantics=("arbitrary",) * n_axes,
                                vmem_limit_bytes=VMEM_LIMIT)


def _rms(xf, g_row):
    ms = jnp.mean(xf * xf, axis=-1, keepdims=True)
    return xf * lax.rsqrt(ms + EPS) * g_row


def _segment_of(i, tile_starts):
    seg = 0
    for start in tile_starts[1:]:
        seg = seg + (i >= start).astype(jnp.int32)
    return seg


def _ffn_body(*refs, in_starts, out_starts):
    n_in, n_out = len(in_starts), len(out_starts)
    x_refs = refs[:n_in]
    pre_g_ref, post_g_ref, wg_ref, wu_ref, wd_ref = refs[n_in:n_in + 5]
    o_refs = refs[n_in + 5:n_in + 5 + n_out]
    h_ref, acc_ref = refs[n_in + 5 + n_out:]
    i = pl.program_id(0)
    f = pl.program_id(1)
    in_seg = _segment_of(i, in_starts)
    out_seg = _segment_of(i, out_starts)

    for k, x_ref in enumerate(x_refs):
        @pl.when(jnp.logical_and(f == 0, in_seg == k))
        def _(x_ref=x_ref):
            h_ref[...] = _rms(x_ref[...], pre_g_ref[...]).astype(BF16)
            acc_ref[...] = jnp.zeros(acc_ref.shape, F32)

    h = h_ref[...]
    g = jnp.dot(h, wg_ref[...], preferred_element_type=F32)
    u = jnp.dot(h, wu_ref[...], preferred_element_type=F32)
    a = (g * jax.nn.sigmoid(g) * u).astype(BF16)
    acc_ref[...] += jnp.dot(a, wd_ref[...], preferred_element_type=F32)

    last = f == pl.num_programs(1) - 1
    for k, x_ref in enumerate(x_refs):
        for m, o_ref in enumerate(o_refs):
            @pl.when(jnp.logical_and(last, jnp.logical_and(in_seg == k, out_seg == m)))
            def _(x_ref=x_ref, o_ref=o_ref):
                o_ref[...] = x_ref[...] + _rms(acc_ref[...], 0.5 * post_g_ref[...])


def _ffn(xs, pre_g, post_g, wg, wu, wd, out_rows, *, tm, tf):
    d = xs[0].shape[1]
    fdim = wg.shape[1]

    def tile_starts(rows):
        assert all(r % tm == 0 for r in rows)
        starts = [0]
        for r in rows[:-1]:
            starts.append(starts[-1] + r // tm)
        return tuple(starts)

    in_rows = tuple(x.shape[0] for x in xs)
    assert sum(in_rows) == sum(out_rows)
    in_starts, out_starts = tile_starts(in_rows), tile_starts(out_rows)

    def seg_spec(start, rows):
        n = rows // tm
        return pl.BlockSpec((tm, d), lambda i, f: (jnp.clip(i - start, 0, n - 1), 0))

    fixed = lambda i, f: (0, 0)
    return pl.pallas_call(
        functools.partial(_ffn_body, in_starts=in_starts, out_starts=out_starts),
        out_shape=tuple(jax.ShapeDtypeStruct((r, d), F32) for r in out_rows),
        grid=(sum(in_rows) // tm, fdim // tf),
        in_specs=[seg_spec(s, r) for s, r in zip(in_starts, in_rows)] + [
            pl.BlockSpec((1, d), fixed),
            pl.BlockSpec((1, d), fixed),
            pl.BlockSpec((d, tf), lambda i, f: (0, f)),
            pl.BlockSpec((d, tf), lambda i, f: (0, f)),
            pl.BlockSpec((tf, d), lambda i, f: (f, 0)),
        ],
        out_specs=tuple(seg_spec(s, r) for s, r in zip(out_starts, out_rows)),
        scratch_shapes=[pltpu.VMEM((tm, d), BF16), pltpu.VMEM((tm, d), F32)],
        compiler_params=_params(2),
        name="ffn",
    )(*xs, pre_g, post_g, wg, wu, wd)


def _inproj_body(x_ref, g_ref, w_ref, cs_ref, wgate_ref, gbias_ref, o_ref, gates_ref, h_ref):
    j = pl.program_id(1)

    @pl.when(j == 0)
    def _():
        h = _rms(x_ref[...], g_ref[...]).astype(BF16)
        h_ref[...] = h
        gates = jnp.dot(h, wgate_ref[...], preferred_element_type=F32) + gbias_ref[...]
        gates_ref[...] = gates.T

    o_ref[...] = (jnp.dot(h_ref[...], w_ref[...], preferred_element_type=F32) * cs_ref[...]).astype(BF16)


def _in_proj(x, g, w_main, col_scale, w_gate, gate_bias, *, tm, tn):
    t, d = x.shape
    n = col_scale.shape[1]
    assert n % tn == 0 and n <= w_main.shape[1]
    return pl.pallas_call(
        _inproj_body,
        out_shape=(jax.ShapeDtypeStruct((t, n), BF16), jax.ShapeDtypeStruct((LANES, t), F32)),
        grid=(t // tm, n // tn),
        in_specs=[
            pl.BlockSpec((tm, d), lambda i, j: (i, 0)),
            pl.BlockSpec((1, d), lambda i, j: (0, 0)),
            pl.BlockSpec((d, tn), lambda i, j: (0, j)),
            pl.BlockSpec((1, tn), lambda i, j: (0, j)),
            pl.BlockSpec((d, LANES), lambda i, j: (0, 0)),
            pl.BlockSpec((1, LANES), lambda i, j: (0, 0)),
        ],
        out_specs=(pl.BlockSpec((tm, tn), lambda i, j: (i, j)),
                   pl.BlockSpec((LANES, tm), lambda i, j: (0, i))),
        scratch_shapes=[pltpu.VMEM((tm, d), BF16)],
        compiler_params=_params(2),
        name="in_proj",
    )(x, g, w_main, col_scale, w_gate, gate_bias)


def _rel_bucket(rel):
    nb = REL_BUCKETS // 2
    max_exact = nb // 2
    ret = jnp.where(rel > 0, nb, 0)
    n = jnp.abs(rel)
    nf = jnp.maximum(n, 1).astype(jnp.float32)
    large = max_exact + (jnp.log(nf / max_exact) / math.log(REL_MAX_DIST / max_exact)
                         * (nb - max_exact)).astype(jnp.int32)
    large = jnp.minimum(large, nb - 1)
    return ret + jnp.where(n < max_exact, n, large)


def _bias_tiles_body(relb_ref, bucket_ref, o_ref):
    hc = pl.program_id(0)
    bucket = bucket_ref[...]
    acc = jnp.zeros(bucket.shape, F32)
    for b in range(REL_BUCKETS):
        acc = jnp.where(bucket == b, relb_ref[b, hc], acc)
    o_ref[...] = acc * LOG2E


def _bias_tiles(rel_bias):
    key = lax.broadcasted_iota(jnp.int32, (3, LANES, LANES), 1)
    query = lax.broadcasted_iota(jnp.int32, (3, LANES, LANES), 2)
    off = (lax.broadcasted_iota(jnp.int32, (3, LANES, LANES), 0) - 1) * LANES
    bucket = _rel_bucket(off + key - query).astype(jnp.int32)
    n_hc = rel_bias.shape[1]
    return pl.pallas_call(
        _bias_tiles_body,
        out_shape=jax.ShapeDtypeStruct((n_hc, 3, LANES, LANES), F32),
        grid=(n_hc,),
        in_specs=[
            pl.BlockSpec(memory_space=pltpu.SMEM),
            pl.BlockSpec((3, LANES, LANES), lambda i: (0, 0, 0)),
        ],
        out_specs=pl.BlockSpec((None, 3, LANES, LANES), lambda i: (i, 0, 0, 0)),
        compiler_params=_params(1),
        name="rel_bias_tiles",
    )(rel_bias, bucket)


V_ROWS = ATT_V_DIM + BF16_ROWS


def _attn_body(relb_ref, lam_ref, q_ref, k_ref, v_ref, btile_ref, hg_ref, o_ref,
               qz_ref, vt_ref, bias_ref, sa_ref, sb_ref, ma_ref, mb_ref, ha_ref, hb_ref, acc_ref, m_ref, *, tile):
    h = pl.program_id(0)
    seq = k_ref.shape[0]
    n_tiles = seq // tile
    n_sub = tile // LANES
    far_bucket_neg = REL_BUCKETS // 2 - 1
    far_bucket_pos = REL_BUCKETS - 1

    q = q_ref[...]
    lane = lax.broadcasted_iota(jnp.int32, q.shape, 1)
    zero = jnp.zeros_like(q)
    qz_ref[0] = jnp.where(lane < ATT_QK_DIM, q, zero)
    qz_ref[1] = jnp.where(lane >= ATT_QK_DIM, q, zero)
    ones_rows = jnp.where(lax.broadcasted_iota(jnp.int32, (V_ROWS - ATT_V_DIM, tile), 0) == 0,
                          1.0, 0.0).astype(BF16)
    for jt in range(n_tiles):
        vt_ref[jt, :ATT_V_DIM, :] = v_ref[jt * tile:(jt + 1) * tile, :].T
        vt_ref[jt, ATT_V_DIM:, :] = ones_rows
    @pl.when(pl.program_id(1) == 0)
    def _():
        for c in range(2):
            cneg = relb_ref[far_bucket_neg, 2 * h + c] * LOG2E
            cpos = relb_ref[far_bucket_pos, 2 * h + c] * LOG2E
            for di in range(3):
                for kj in range(n_sub):
                    for qi in range(n_sub):
                        d = (di - 1) * n_sub + kj - qi
                        if -1 <= d <= 1:
                            sub = btile_ref[c, d + 1]
                        else:
                            sub = jnp.full((LANES, LANES), cneg if d < 0 else cpos, F32)
                        bias_ref[c, di, kj * LANES:(kj + 1) * LANES, qi * LANES:(qi + 1) * LANES] = sub

    m_ref[...] = jnp.full(m_ref.shape, -jnp.inf, F32)
    acc_ref[...] = jnp.zeros(acc_ref.shape, F32)

    lam_v = lam_ref[...]
    s1 = jnp.sum(lam_v[0:1] * lam_v[1:2], axis=-1, keepdims=True)
    s2 = jnp.sum(lam_v[2:3] * lam_v[3:4], axis=-1, keepdims=True)
    lam = jnp.exp(s1) - jnp.exp(s2) + LAMBDA_INIT

    def scores(qi, j, bufs, near):
        s_ref, smax_ref, shift_ref = bufs
        kt = k_ref[pl.ds(pl.multiple_of(j * tile, tile), tile), :]
        for c in range(2):
            qz = qz_ref[c, pl.ds(pl.multiple_of(qi * tile, tile), tile), :]
            s = lax.dot_general(kt, qz, (((1,), (1,)), ((), ())), preferred_element_type=F32)
            if near:
                s = s + bias_ref[c, j - qi + 1]
                shift = jnp.zeros((1, tile), F32)
            else:
                shift = jnp.full((1, tile), jnp.where(j < qi, relb_ref[far_bucket_neg, 2 * h + c],
                                                      relb_ref[far_bucket_pos, 2 * h + c]) * LOG2E, F32)
            s_ref[c] = s
            smax_ref[c] = jnp.max(s, axis=0, keepdims=True) + shift
            shift_ref[c] = shift

    def accumulate(j, bufs):
        s_ref, smax_ref, shift_ref = bufs
        vt = vt_ref[j]
        for c in range(2):
            m_old = m_ref[c]
            m_new = jnp.maximum(m_old, smax_ref[c])
            alpha = jnp.exp2(m_old - m_new)
            e = jnp.exp2(s_ref[c] - (m_new - shift_ref[c])).astype(BF16)
            acc_ref[c] = alpha * acc_ref[c] + jnp.dot(vt, e, preferred_element_type=F32)
            m_ref[c] = m_new

    def finalize(qi):
        outs = []
        for c in range(2):
            acc = acc_ref[c]
            outs.append(acc[:ATT_V_DIM] * (1.0 / acc[ATT_V_DIM:ATT_V_DIM + 1]))
        o_t = outs[0] - lam * outs[1]
        y_t = o_t * lax.rsqrt(jnp.mean(o_t * o_t, axis=0, keepdims=True) + EPS)
        o_ref[pl.ds(pl.multiple_of(qi * tile, tile), tile), :] = (
            y_t.T * hg_ref[...] * (1.0 - LAMBDA_INIT)).astype(BF16)
        m_ref[...] = jnp.full(m_ref.shape, -jnp.inf, F32)
        acc_ref[...] = jnp.zeros(acc_ref.shape, F32)

    buf_a = (sa_ref, ma_ref, ha_ref)
    buf_b = (sb_ref, mb_ref, hb_ref)

    def is_near(qi, j):
        return jnp.abs(j - qi) <= 1

    def pair(q1, j1, q2, j2, jc1, jc2, tail=None):
        n1, n2 = is_near(q1, j1), is_near(q2, j2)
        for v1 in (True, False):
            for v2 in (True, False):
                @pl.when(jnp.logical_and(n1 == v1, n2 == v2))
                def _(v1=v1, v2=v2):
                    scores(q1, j1, buf_b, v1)
                    accumulate(jc1, buf_a)
                    scores(q2, j2, buf_a, v2)
                    accumulate(jc2, buf_b)
                    if tail is not None:
                        tail()

    scores(0, 0, buf_a, True)

    def q_body(qi, carry):
        def pair_body(t, c2):
            j = 2 * t
            pair(qi, j + 1, qi, j + 2, j, j + 1)
            return c2

        lax.fori_loop(0, n_tiles // 2 - 1, pair_body, 0)
        pair(qi, n_tiles - 1, jnp.minimum(qi + 1, n_tiles - 1), 0, n_tiles - 2, n_tiles - 1,
             tail=lambda: finalize(qi))
        return carry

    lax.fori_loop(0, n_tiles, q_body, 0)


def _attention(proj, btiles, rel_bias, lam_vecs, head_g, *, tile):
    nb, seq, _ = proj.shape
    assert seq % (2 * tile) == 0 and tile % LANES == 0
    btiles = btiles.reshape(ATT_HEADS, 2, 3, LANES, LANES)
    return pl.pallas_call(
        functools.partial(_attn_body, tile=tile),
        out_shape=jax.ShapeDtypeStruct((nb, seq, ATT_WIDTH), BF16),
        grid=(ATT_HEADS, nb),
        in_specs=[
            pl.BlockSpec(memory_space=pltpu.SMEM),
            pl.BlockSpec((4, ATT_QK_DIM), lambda h, b: (0, 0)),
            pl.BlockSpec((None, seq, LANES), lambda h, b: (b, 0, h)),
            pl.BlockSpec((None, seq, LANES), lambda h, b: (b, 0, ATT_HEADS + h)),
            pl.BlockSpec((None, seq, LANES), lambda h, b: (b, 0, 2 * ATT_HEADS + h)),
            pl.BlockSpec((None, 2, 3, LANES, LANES), lambda h, b: (h, 0, 0, 0, 0)),
            pl.BlockSpec((1, ATT_V_DIM), lambda h, b: (0, h)),
        ],
        out_specs=pl.BlockSpec((None, seq, ATT_V_DIM), lambda h, b: (b, 0, h)),
        scratch_shapes=[
            pltpu.VMEM((2, seq, LANES), BF16),
            pltpu.VMEM((seq // tile, V_ROWS, tile), BF16),
            pltpu.VMEM((2, 3, tile, tile), F32),
            pltpu.VMEM((2, tile, tile), F32),
            pltpu.VMEM((2, tile, tile), F32),
            pltpu.VMEM((2, 1, tile), F32),
            pltpu.VMEM((2, 1, tile), F32),
            pltpu.VMEM((2, 1, tile), F32),
            pltpu.VMEM((2, 1, tile), F32),
            pltpu.VMEM((2, V_ROWS, tile), F32),
            pltpu.VMEM((2, 1, tile), F32),
        ],
        compiler_params=_params(2),
        name="diff_attention",
    )(rel_bias, lam_vecs, proj, proj, proj, btiles, head_g)


def _log_sigmoid(x):
    return jnp.minimum(x, 0.0) - jnp.log1p(jnp.exp(-jnp.abs(x)))


def _lane_scan(x, reverse):
    lane = lax.broadcasted_iota(jnp.int32, x.shape, 1)
    sh = 1
    while sh < LANES:
        if reverse:
            x = x + jnp.where(lane < LANES - sh, pltpu.roll(x, LANES - sh, 1), 0.0)
        else:
            x = x + jnp.where(lane >= sh, pltpu.roll(x, sh, 1), 0.0)
        sh *= 2
    return x


def _mlstm_body(mq_ref, mk_ref, mv_ref, mo_ref, cwq_ref, cwk_ref, cbq_ref, cbk_ref, gates_ref, hg_ref,
                o_ref, q_s, kt_s, gsc, hbuf, c_s):
    seq, dh = mq_ref.shape
    L = MLSTM_CHUNK
    nc = seq // L
    halo = BF16_ROWS

    def conv_chunk(src_ref, w_ref, b_ref, c):
        start = pl.multiple_of(c * L, L)
        cur = src_ref[pl.ds(start, L), :].astype(F32)
        prev_start = pl.multiple_of(jnp.maximum(start - halo, 0), halo)
        next_start = pl.multiple_of(jnp.minimum(start + L, seq - halo), halo)
        prev = src_ref[pl.ds(prev_start, halo), :].astype(F32) * jnp.where(c > 0, 1.0, 0.0)
        nxt = src_ref[pl.ds(next_start, halo), :].astype(F32) * jnp.where(c < nc - 1, 1.0, 0.0)
        ext = jnp.concatenate([prev, cur, nxt], axis=0)
        w = w_ref[...]
        out = jnp.broadcast_to(b_ref[...], (L, dh))
        pad = CONV_WIDTH // 2
        for t in range(CONV_WIDTH):
            lo = halo + t - pad
            out = out + ext[lo:lo + L, :] * w[t:t + 1, :]
        return out * jax.nn.sigmoid(out)

    def prep(c, carry):
        start = pl.multiple_of(c * L, L)
        qc = conv_chunk(mq_ref, cwq_ref, cbq_ref, c) * (MLSTM_HEAD_DIM ** -0.5)
        q_s[pl.ds(start, L), :] = qc.astype(BF16)
        kc = conv_chunk(mk_ref, cwk_ref, cbk_ref, c)
        kt_s[c] = kc.T.astype(BF16)
        return carry

    lax.fori_loop(0, nc, prep, 0)

    g = gates_ref[...]
    b_f = _lane_scan(_log_sigmoid(g[2]), reverse=False)
    b_b = _lane_scan(_log_sigmoid(g[3]), reverse=True)
    gsc[0] = b_f
    gsc[1] = g[0] - b_f
    gsc[2] = b_b
    gsc[3] = g[1] - b_b

    c_s[...] = jnp.zeros(c_s.shape, F32)

    row_i = lax.broadcasted_iota(jnp.int32, (L, L), 0)
    col_i = lax.broadcasted_iota(jnp.int32, (L, L), 1)
    lane_row = lax.broadcasted_iota(jnp.int32, (1, L), 1)
    ones_col = jnp.where(lax.broadcasted_iota(jnp.int32, (L, LANES), 1) == 0, 1.0, 0.0).astype(BF16)

    def chunk_of(k, direction):
        return k if direction == 0 else nc - 1 - k

    def chunk_step(k, m, direction):
        c = chunk_of(k, direction)
        start = pl.multiple_of(c * L, L)
        causal = (col_i <= row_i) if direction == 0 else (col_i >= row_i)
        last_lane = L - 1 if direction == 0 else 0
        q = q_s[pl.ds(start, L), :]
        kt = kt_s[c]
        vaug = jnp.concatenate([mv_ref[pl.ds(start, L), :], ones_col], axis=1)
        brow = gsc[2 * direction, pl.ds(c, 1), :]
        rrow = gsc[2 * direction + 1, pl.ds(c, 1), :]
        bcol = jnp.sum(jnp.where(row_i == col_i, brow, 0.0), axis=1, keepdims=True)
        dmat = jnp.where(causal, bcol + rrow, -jnp.inf)
        dmax = jnp.max(dmat, axis=1, keepdims=True)
        c_old = c_s[direction]
        q_out = jnp.dot(q, jnp.concatenate([kt, c_old.astype(BF16)], axis=1),
                        preferred_element_type=F32)
        qk, p_inter = q_out[:, :L], q_out[:, L:]
        st = (qk * jnp.exp(dmat - dmax)).astype(BF16)
        b_last = jnp.sum(jnp.where(lane_row == last_lane, brow, 0.0), axis=1, keepdims=True)
        grow = b_last + rrow
        gmax = jnp.max(grow, axis=1, keepdims=True)
        wk = jnp.exp(grow - gmax)
        ktw = (kt.astype(F32) * wk).astype(BF16)
        v_out = jnp.dot(jnp.concatenate([st, ktw], axis=0), vaug, preferred_element_type=F32)
        p_intra, upd = v_out[:L], v_out[L:]
        inter = bcol + m
        m_t = jnp.maximum(inter, dmax)
        num_aug = jnp.exp(inter - m_t) * p_inter + jnp.exp(dmax - m_t) * p_intra
        den = num_aug[:, dh:dh + 1]
        hout = num_aug[:, :dh] / jnp.maximum(jnp.abs(den), jnp.exp(-m_t))
        m_new = jnp.maximum(b_last + m, gmax)
        c_s[direction] = jnp.exp(b_last + m - m_new) * c_old + jnp.exp(gmax - m_new) * upd
        return hout, m_new

    def finalize(c, hm):
        start = pl.multiple_of(c * L, L)
        y = _rms(hm, hg_ref[...])
        og = jax.nn.sigmoid(mo_ref[pl.ds(start, L), :].astype(F32))
        o_ref[pl.ds(start, L), :] = (og * y).astype(BF16)

    def run_half(first_step, ms, second_half):
        def body(k, ms):
            out = []
            for direction in range(2):
                hout, m_new = chunk_step(k, ms[direction], direction)
                rows = pl.ds(pl.multiple_of(chunk_of(k, direction) * L, L), L)
                if second_half:
                    finalize(chunk_of(k, direction), hout + hbuf[rows, :])
                else:
                    hbuf[rows, :] = hout
                out.append(m_new)
            return tuple(out)

        return lax.fori_loop(first_step, first_step + nc // 2, body, ms, unroll=2)

    m0 = jnp.zeros((1, 1), F32)
    ms = run_half(0, (m0, m0), False)
    run_half(nc // 2, ms, True)


def _mlstm(proj, gates, conv_w, conv_b, head_g):
    nb, seq, _ = proj.shape
    dh = MLSTM_HEAD_DIM
    nc = seq // MLSTM_CHUNK
    assert seq % MLSTM_CHUNK == 0 and nc % 2 == 0
    base = 3 * ATT_WIDTH // dh
    nh = MLSTM_HEADS

    def col(group):
        return pl.BlockSpec((None, seq, dh), lambda b, h: (b, 0, base + group * nh + h))

    return pl.pallas_call(
        _mlstm_body,
        out_shape=jax.ShapeDtypeStruct((nb, seq, MLSTM_WIDTH), BF16),
        grid=(nb, nh),
        in_specs=[
            col(0), col(1), col(2), col(3),
            pl.BlockSpec((CONV_WIDTH, dh), lambda b, h: (0, h)),
            pl.BlockSpec((CONV_WIDTH, dh), lambda b, h: (0, nh + h)),
            pl.BlockSpec((1, dh), lambda b, h: (0, h)),
            pl.BlockSpec((1, dh), lambda b, h: (0, nh + h)),
            pl.BlockSpec((4, None, None, nc, MLSTM_CHUNK), lambda b, h: (0, h, b, 0, 0)),
            pl.BlockSpec((1, dh), lambda b, h: (0, h)),
        ],
        out_specs=pl.BlockSpec((None, seq, dh), lambda b, h: (b, 0, h)),
        scratch_shapes=[
            pltpu.VMEM((seq, dh), BF16),
            pltpu.VMEM((nc, dh, MLSTM_CHUNK), BF16),
            pltpu.VMEM((4, nc, MLSTM_CHUNK), F32),
            pltpu.VMEM((seq, dh), F32),
            pltpu.VMEM((2, dh, dh + LANES), F32),
        ],
        compiler_params=_params(2),
        name="mlstm",
    )(proj, proj, proj, proj, conv_w, conv_w, conv_b, conv_b, gates, head_g)


def _outproj_body(att_ref, ml_ref, wa_ref, wm_ref, x_ref, g_ref, o_ref):
    mixed = (jnp.dot(att_ref[...], wa_ref[...], preferred_element_type=F32)
             + jnp.dot(ml_ref[...], wm_ref[...], preferred_element_type=F32))
    o_ref[...] = x_ref[...] + _rms(mixed, g_ref[...])


def _out_proj(att, ml, w_out, x, g, *, tm):
    t, d = x.shape
    wa = att.shape[1]
    wm = ml.shape[1]
    assert wa == wm
    return pl.pallas_call(
        _outproj_body,
        out_shape=jax.ShapeDtypeStruct((t, d), F32),
        grid=(t // tm,),
        in_specs=[
            pl.BlockSpec((tm, wa), lambda i: (i, 0)),
            pl.BlockSpec((tm, wm), lambda i: (i, 0)),
            pl.BlockSpec((wa, d), lambda i: (0, 0)),
            pl.BlockSpec((wm, d), lambda i: (1, 0)),
            pl.BlockSpec((tm, d), lambda i: (i, 0)),
            pl.BlockSpec((1, d), lambda i: (0, 0)),
        ],
        out_specs=pl.BlockSpec((tm, d), lambda i: (i, 0)),
        compiler_params=_params(1),
        name="out_proj",
    )(att, ml, w_out, w_out, x, g)


def _layer(xs, rel_bias, ffn1_pre_g, ffn1_post_g, ffn1_w_gate, ffn1_w_up, ffn1_w_down,
           mix_pre_g, mix_post_g, w_in, gate_bias, conv_w, conv_b, lam_vecs,
           att_head_g, mlstm_head_g, w_out,
           ffn2_pre_g, ffn2_post_g, ffn2_w_gate, ffn2_w_up, ffn2_w_down,
           *, tm, tm_proj, tf, tn, attn_tile):
    seq, d = xs[0].shape[1:]
    rows = tuple(x.shape[0] * seq for x in xs)
    t = sum(rows)
    nb = t // seq
    row = lambda v: v.reshape(1, -1).astype(F32)

    (x1,) = _ffn(tuple(x.reshape(-1, d) for x in xs), row(ffn1_pre_g), row(ffn1_post_g),
                 ffn1_w_gate.astype(BF16), ffn1_w_up.astype(BF16), ffn1_w_down.astype(BF16), (t,),
                 tm=tm, tf=tf)

    w_main = w_in.astype(BF16)
    w_gate = jnp.pad(w_main[:, MAIN_COLS:], ((0, 0), (0, LANES - N_GATES)))
    gbias = jnp.pad(row(gate_bias), ((0, 0), (0, LANES - N_GATES)))
    col_scale = jnp.concatenate([jnp.full((1, ATT_WIDTH), ATT_QK_DIM ** -0.5 * LOG2E, F32),
                                 jnp.ones((1, MAIN_COLS - ATT_WIDTH), F32)], axis=1)
    proj, gates = _in_proj(x1, row(mix_pre_g), w_main, col_scale, w_gate, gbias, tm=tm_proj, tn=tn)
    proj = proj.reshape(nb, seq, MAIN_COLS)
    nc = seq // MLSTM_CHUNK
    gates = gates[:N_GATES].reshape(N_GATES // MLSTM_HEADS, MLSTM_HEADS, nb, nc, MLSTM_CHUNK)

    btiles = _bias_tiles(rel_bias.astype(F32))
    att = _attention(proj, btiles, rel_bias.astype(F32), lam_vecs, row(att_head_g), tile=attn_tile)
    ml = _mlstm(proj, gates, conv_w.astype(F32), row(conv_b), row(mlstm_head_g))

    x2 = _out_proj(att.reshape(t, ATT_WIDTH), ml.reshape(t, MLSTM_WIDTH), w_out.astype(BF16),
                   x1, row(mix_post_g), tm=tm)
    ys = _ffn((x2,), row(ffn2_pre_g), row(ffn2_post_g), ffn2_w_gate.astype(BF16),
              ffn2_w_up.astype(BF16), ffn2_w_down.astype(BF16), rows, tm=tm, tf=tf)
    return tuple(y.reshape(x.shape) for y, x in zip(ys, xs))


def kernel(x_prompt, x_sample, rel_bias, ffn1_pre_g, ffn1_post_g, ffn1_w_gate, ffn1_w_up, ffn1_w_down, mix_pre_g, mix_post_g, w_in, gate_bias, conv_w, conv_b, lambda_q1, lambda_k1, lambda_q2, lambda_k2, att_head_g, mlstm_head_g, w_out, ffn2_pre_g, ffn2_post_g, ffn2_w_gate, ffn2_w_up, ffn2_w_down, *, tm=512, tm_proj=1024, tf=512, tn=1792, attn_tile=512):
    assert x_prompt.shape[1:] == x_sample.shape[1:]
    assert ffn1_pre_g.shape[0] == 1, "single-layer trunk"
    lam_vecs = jnp.concatenate([lambda_q1, lambda_k1, lambda_q2, lambda_k2], axis=0).astype(F32)
    return _layer((x_prompt, x_sample), rel_bias, ffn1_pre_g[0], ffn1_post_g[0], ffn1_w_gate[0], ffn1_w_up[0], ffn1_w_down[0],
               mix_pre_g[0], mix_post_g[0], w_in[0], gate_bias[0], conv_w[0], conv_b[0], lam_vecs,
               att_head_g[0], mlstm_head_g[0], w_out[0],
               ffn2_pre_g[0], ffn2_post_g[0], ffn2_w_gate[0], ffn2_w_up[0], ffn2_w_down[0],
               tm=tm, tm_proj=tm_proj, tf=tf, tn=tn, attn_tile=attn_tile)
```

```python
import functools
import math

import jax
import jax.numpy as jnp
from jax import lax
from jax.experimental import pallas as pl
from jax.experimental.pallas import tpu as pltpu

F32 = jnp.float32
BF16 = jnp.bfloat16

EPS = 1e-6
ATT_HEADS = 8
ATT_QK_DIM = 64
ATT_V_DIM = 128
ATT_WIDTH = ATT_HEADS * ATT_V_DIM
MLSTM_HEADS = 4
MLSTM_HEAD_DIM = 256
MLSTM_WIDTH = MLSTM_HEADS * MLSTM_HEAD_DIM
MLSTM_CHUNK = 128
CONV_WIDTH = 5
N_GATES = 4 * MLSTM_HEADS
MAIN_COLS = 3 * ATT_WIDTH + 4 * MLSTM_WIDTH
REL_BUCKETS = 32
REL_MAX_DIST = 128
LAMBDA_INIT = 0.8 - 0.6 * math.exp(-0.3 * 0)
LOG2E = math.log2(math.e)

LANES = 128
BF16_ROWS = 16
VMEM_LIMIT = 56 * 1024 * 1024
assert REL_MAX_DIST <= LANES


def _params(n_axes):
    return pltpu.CompilerParams(dimension_semantics=("arbitrary",) * n_axes,
                                vmem_limit_bytes=VMEM_LIMIT)


def _rms(xf, g_row):
    ms = jnp.mean(xf * xf, axis=-1, keepdims=True)
    return xf * lax.rsqrt(ms + EPS) * g_row


def _segment_of(i, tile_starts):
    seg = 0
    for start in tile_starts[1:]:
        seg = seg + (i >= start).astype(jnp.int32)
    return seg


def _ffn_body(*refs, in_starts, out_starts):
    n_in, n_out = len(in_starts), len(out_starts)
    x_refs = refs[:n_in]
    pre_g_ref, post_g_ref, wg_ref, wu_ref, wd_ref = refs[n_in:n_in + 5]
    o_refs = refs[n_in + 5:n_in + 5 + n_out]
    h_ref, acc_ref = refs[n_in + 5 + n_out:]
    i = pl.program_id(0)
    f = pl.program_id(1)
    in_seg = _segment_of(i, in_starts)
    out_seg = _segment_of(i, out_starts)

    for k, x_ref in enumerate(x_refs):
        @pl.when(jnp.logical_and(f == 0, in_seg == k))
        def _(x_ref=x_ref):
            h_ref[...] = _rms(x_ref[...], pre_g_ref[...]).astype(BF16)
            acc_ref[...] = jnp.zeros(acc_ref.shape, F32)

    h = h_ref[...]
    g = jnp.dot(h, wg_ref[...], preferred_element_type=F32)
    u = jnp.dot(h, wu_ref[...], preferred_element_type=F32)
    a = (g * jax.nn.sigmoid(g) * u).astype(BF16)
    acc_ref[...] += jnp.dot(a, wd_ref[...], preferred_element_type=F32)

    last = f == pl.num_programs(1) - 1
    for k, x_ref in enumerate(x_refs):
        for m, o_ref in enumerate(o_refs):
            @pl.when(jnp.logical_and(last, jnp.logical_and(in_seg == k, out_seg == m)))
            def _(x_ref=x_ref, o_ref=o_ref):
                o_ref[...] = x_ref[...] + _rms(acc_ref[...], 0.5 * post_g_ref[...])


def _ffn(xs, pre_g, post_g, wg, wu, wd, out_rows, *, tm, tf):
    d = xs[0].shape[1]
    fdim = wg.shape[1]

    def tile_starts(rows):
        assert all(r % tm == 0 for r in rows)
        starts = [0]
        for r in rows[:-1]:
            starts.append(starts[-1] + r // tm)
        return tuple(starts)

    in_rows = tuple(x.shape[0] for x in xs)
    assert sum(in_rows) == sum(out_rows)
    in_starts, out_starts = tile_starts(in_rows), tile_starts(out_rows)

    def seg_spec(start, rows):
        n = rows // tm
        return pl.BlockSpec((tm, d), lambda i, f: (jnp.clip(i - start, 0, n - 1), 0))

    fixed = lambda i, f: (0, 0)
    return pl.pallas_call(
        functools.partial(_ffn_body, in_starts=in_starts, out_starts=out_starts),
        out_shape=tuple(jax.ShapeDtypeStruct((r, d), F32) for r in out_rows),
        grid=(sum(in_rows) // tm, fdim // tf),
        in_specs=[seg_spec(s, r) for s, r in zip(in_starts, in_rows)] + [
            pl.BlockSpec((1, d), fixed),
            pl.BlockSpec((1, d), fixed),
            pl.BlockSpec((d, tf), lambda i, f: (0, f)),
            pl.BlockSpec((d, tf), lambda i, f: (0, f)),
            pl.BlockSpec((tf, d), lambda i, f: (f, 0)),
        ],
        out_specs=tuple(seg_spec(s, r) for s, r in zip(out_starts, out_rows)),
        scratch_shapes=[pltpu.VMEM((tm, d), BF16), pltpu.VMEM((tm, d), F32)],
        compiler_params=_params(2),
        name="ffn",
    )(*xs, pre_g, post_g, wg, wu, wd)


def _inproj_body(x_ref, g_ref, w_ref, cs_ref, wgate_ref, gbias_ref, o_ref, gates_ref, h_ref):
    j = pl.program_id(1)

    @pl.when(j == 0)
    def _():
        h = _rms(x_ref[...], g_ref[...]).astype(BF16)
        h_ref[...] = h
        gates = jnp.dot(h, wgate_ref[...], preferred_element_type=F32) + gbias_ref[...]
        gates_ref[...] = gates.T

    o_ref[...] = (jnp.dot(h_ref[...], w_ref[...], preferred_element_type=F32) * cs_ref[...]).astype(BF16)


def _in_proj(x, g, w_main, col_scale, w_gate, gate_bias, *, tm, tn):
    t, d = x.shape
    n = col_scale.shape[1]
    assert n % tn == 0 and n <= w_main.shape[1]
    return pl.pallas_call(
        _inproj_body,
        out_shape=(jax.ShapeDtypeStruct((t, n), BF16), jax.ShapeDtypeStruct((LANES, t), F32)),
        grid=(t // tm, n // tn),
        in_specs=[
            pl.BlockSpec((tm, d), lambda i, j: (i, 0)),
            pl.BlockSpec((1, d), lambda i, j: (0, 0)),
            pl.BlockSpec((d, tn), lambda i, j: (0, j)),
            pl.BlockSpec((1, tn), lambda i, j: (0, j)),
            pl.BlockSpec((d, LANES), lambda i, j: (0, 0)),
            pl.BlockSpec((1, LANES), lambda i, j: (0, 0)),
        ],
        out_specs=(pl.BlockSpec((tm, tn), lambda i, j: (i, j)),
                   pl.BlockSpec((LANES, tm), lambda i, j: (0, i))),
        scratch_shapes=[pltpu.VMEM((tm, d), BF16)],
        compiler_params=_params(2),
        name="in_proj",
    )(x, g, w_main, col_scale, w_gate, gate_bias)


def _rel_bucket(rel):
    nb = REL_BUCKETS // 2
    max_exact = nb // 2
    ret = jnp.where(rel > 0, nb, 0)
    n = jnp.abs(rel)
    nf = jnp.maximum(n, 1).astype(jnp.float32)
    large = max_exact + (jnp.log(nf / max_exact) / math.log(REL_MAX_DIST / max_exact)
                         * (nb - max_exact)).astype(jnp.int32)
    large = jnp.minimum(large, nb - 1)
    return ret + jnp.where(n < max_exact, n, large)


def _bias_tiles_body(relb_ref, bucket_ref, o_ref):
    hc = pl.program_id(0)
    bucket = bucket_ref[...]
    acc = jnp.zeros(bucket.shape, F32)
    for b in range(REL_BUCKETS):
        acc = jnp.where(bucket == b, relb_ref[b, hc], acc)
    o_ref[...] = acc * LOG2E


def _bias_tiles(rel_bias):
    key = lax.broadcasted_iota(jnp.int32, (3, LANES, LANES), 1)
    query = lax.broadcasted_iota(jnp.int32, (3, LANES, LANES), 2)
    off = (lax.broadcasted_iota(jnp.int32, (3, LANES, LANES), 0) - 1) * LANES
    bucket = _rel_bucket(off + key - query).astype(jnp.int32)
    n_hc = rel_bias.shape[1]
    return pl.pallas_call(
        _bias_tiles_body,
        out_shape=jax.ShapeDtypeStruct((n_hc, 3, LANES, LANES), F32),
        grid=(n_hc,),
        in_specs=[
            pl.BlockSpec(memory_space=pltpu.SMEM),
            pl.BlockSpec((3, LANES, LANES), lambda i: (0, 0, 0)),
        ],
        out_specs=pl.BlockSpec((None, 3, LANES, LANES), lambda i: (i, 0, 0, 0)),
        compiler_params=_params(1),
        name="rel_bias_tiles",
    )(rel_bias, bucket)


V_ROWS = ATT_V_DIM + BF16_ROWS


def _attn_body(relb_ref, lam_ref, q_ref, k_ref, v_ref, btile_ref, hg_ref, o_ref,
               qz_ref, vt_ref, bias_ref, sa_ref, sb_ref, ma_ref, mb_ref, ha_ref, hb_ref, acc_ref, m_ref, *, tile):
    h = pl.program_id(0)
    seq = k_ref.shape[0]
    n_tiles = seq // tile
    n_sub = tile // LANES
    far_bucket_neg = REL_BUCKETS // 2 - 1
    far_bucket_pos = REL_BUCKETS - 1

    q = q_ref[...]
    lane = lax.broadcasted_iota(jnp.int32, q.shape, 1)
    zero = jnp.zeros_like(q)
    qz_ref[0] = jnp.where(lane < ATT_QK_DIM, q, zero)
    qz_ref[1] = jnp.where(lane >= ATT_QK_DIM, q, zero)
    ones_rows = jnp.where(lax.broadcasted_iota(jnp.int32, (V_ROWS - ATT_V_DIM, tile), 0) == 0,
                          1.0, 0.0).astype(BF16)
    for jt in range(n_tiles):
        vt_ref[jt, :ATT_V_DIM, :] = v_ref[jt * tile:(jt + 1) * tile, :].T
        vt_ref[jt, ATT_V_DIM:, :] = ones_rows
    @pl.when(pl.program_id(1) == 0)
    def _():
        for c in range(2):
            cneg = relb_ref[far_bucket_neg, 2 * h + c] * LOG2E
            cpos = relb_ref[far_bucket_pos, 2 * h + c] * LOG2E
            for di in range(3):
                for kj in range(n_sub):
                    for qi in range(n_sub):
                        d = (di - 1) * n_sub + kj - qi
                        if -1 <= d <= 1:
                            sub = btile_ref[c, d + 1]
                        else:
                            sub = jnp.full((LANES, LANES), cneg if d < 0 else cpos, F32)
                        bias_ref[c, di, kj * LANES:(kj + 1) * LANES, qi * LANES:(qi + 1) * LANES] = sub

    m_ref[...] = jnp.full(m_ref.shape, -jnp.inf, F32)
    acc_ref[...] = jnp.zeros(acc_ref.shape, F32)

    lam_v = lam_ref[...]
    s1 = jnp.sum(lam_v[0:1] * lam_v[1:2], axis=-1, keepdims=True)
    s2 = jnp.sum(lam_v[2:3] * lam_v[3:4], axis=-1, keepdims=True)
    lam = jnp.exp(s1) - jnp.exp(s2) + LAMBDA_INIT

    def scores(qi, j, bufs, near):
        s_ref, smax_ref, shift_ref = bufs
        kt = k_ref[pl.ds(pl.multiple_of(j * tile, tile), tile), :]
        for c in range(2):
            qz = qz_ref[c, pl.ds(pl.multiple_of(qi * tile, tile), tile), :]
            s = lax.dot_general(kt, qz, (((1,), (1,)), ((), ())), preferred_element_type=F32)
            if near:
                s = s + bias_ref[c, j - qi + 1]
                shift = jnp.zeros((1, tile), F32)
            else:
                shift = jnp.full((1, tile), jnp.where(j < qi, relb_ref[far_bucket_neg, 2 * h + c],
                                                      relb_ref[far_bucket_pos, 2 * h + c]) * LOG2E, F32)
            s_ref[c] = s
            smax_ref[c] = jnp.max(s, axis=0, keepdims=True) + shift
            shift_ref[c] = shift

    def accumulate(j, bufs):
        s_ref, smax_ref, shift_ref = bufs
        vt = vt_ref[j]
        for c in range(2):
            m_old = m_ref[c]
            m_new = jnp.maximum(m_old, smax_ref[c])
            alpha = jnp.exp2(m_old - m_new)
            e = jnp.exp2(s_ref[c] - (m_new - shift_ref[c])).astype(BF16)
            acc_ref[c] = alpha * acc_ref[c] + jnp.dot(vt, e, preferred_element_type=F32)
            m_ref[c] = m_new

    def finalize(qi):
        outs = []
        for c in range(2):
            acc = acc_ref[c]
            outs.append(acc[:ATT_V_DIM] / acc[ATT_V_DIM:ATT_V_DIM + 1])
        o_t = outs[0] - lam * outs[1]
        y_t = o_t * lax.rsqrt(jnp.mean(o_t * o_t, axis=0, keepdims=True) + EPS)
        o_ref[pl.ds(pl.multiple_of(qi * tile, tile), tile), :] = (
            y_t.T * hg_ref[...] * (1.0 - LAMBDA_INIT)).astype(BF16)
        m_ref[...] = jnp.full(m_ref.shape, -jnp.inf, F32)
        acc_ref[...] = jnp.zeros(acc_ref.shape, F32)

    buf_a = (sa_ref, ma_ref, ha_ref)
    buf_b = (sb_ref, mb_ref, hb_ref)

    def is_near(qi, j):
        return jnp.abs(j - qi) <= 1

    def pair(q1, j1, q2, j2, jc1, jc2, tail=None):
        n1, n2 = is_near(q1, j1), is_near(q2, j2)
        for v1 in (True, False):
            for v2 in (True, False):
                @pl.when(jnp.logical_and(n1 == v1, n2 == v2))
                def _(v1=v1, v2=v2):
                    scores(q1, j1, buf_b, v1)
                    accumulate(jc1, buf_a)
                    scores(q2, j2, buf_a, v2)
                    accumulate(jc2, buf_b)
                    if tail is not None:
                        tail()

    scores(0, 0, buf_a, True)

    def q_body(qi, carry):
        def pair_body(t, c2):
            j = 2 * t
            pair(qi, j + 1, qi, j + 2, j, j + 1)
            return c2

        lax.fori_loop(0, n_tiles // 2 - 1, pair_body, 0)
        pair(qi, n_tiles - 1, jnp.minimum(qi + 1, n_tiles - 1), 0, n_tiles - 2, n_tiles - 1,
             tail=lambda: finalize(qi))
        return carry

    lax.fori_loop(0, n_tiles, q_body, 0)


def _attention(proj, btiles, rel_bias, lam_vecs, head_g, *, tile):
    nb, seq, _ = proj.shape
    assert seq % (2 * tile) == 0 and tile % LANES == 0
    btiles = btiles.reshape(ATT_HEADS, 2, 3, LANES, LANES)
    return pl.pallas_call(
        functools.partial(_attn_body, tile=tile),
        out_shape=jax.ShapeDtypeStruct((nb, seq, ATT_WIDTH), BF16),
        grid=(ATT_HEADS, nb),
        in_specs=[
            pl.BlockSpec(memory_space=pltpu.SMEM),
            pl.BlockSpec((4, ATT_QK_DIM), lambda h, b: (0, 0)),
            pl.BlockSpec((None, seq, LANES), lambda h, b: (b, 0, h)),
            pl.BlockSpec((None, seq, LANES), lambda h, b: (b, 0, ATT_HEADS + h)),
            pl.BlockSpec((None, seq, LANES), lambda h, b: (b, 0, 2 * ATT_HEADS + h)),
            pl.BlockSpec((None, 2, 3, LANES, LANES), lambda h, b: (h, 0, 0, 0, 0)),
            pl.BlockSpec((1, ATT_V_DIM), lambda h, b: (0, h)),
        ],
        out_specs=pl.BlockSpec((None, seq, ATT_V_DIM), lambda h, b: (b, 0, h)),
        scratch_shapes=[
            pltpu.VMEM((2, seq, LANES), BF16),
            pltpu.VMEM((seq // tile, V_ROWS, tile), BF16),
            pltpu.VMEM((2, 3, tile, tile), F32),
            pltpu.VMEM((2, tile, tile), F32),
            pltpu.VMEM((2, tile, tile), F32),
            pltpu.VMEM((2, 1, tile), F32),
            pltpu.VMEM((2, 1, tile), F32),
            pltpu.VMEM((2, 1, tile), F32),
            pltpu.VMEM((2, 1, tile), F32),
            pltpu.VMEM((2, V_ROWS, tile), F32),
            pltpu.VMEM((2, 1, tile), F32),
        ],
        compiler_params=_params(2),
        name="diff_attention",
    )(rel_bias, lam_vecs, proj, proj, proj, btiles, head_g)


def _log_sigmoid(x):
    return jnp.minimum(x, 0.0) - jnp.log1p(jnp.exp(-jnp.abs(x)))


def _lane_scan(x, reverse):
    lane = lax.broadcasted_iota(jnp.int32, x.shape, 1)
    sh = 1
    while sh < LANES:
        if reverse:
            x = x + jnp.where(lane < LANES - sh, pltpu.roll(x, LANES - sh, 1), 0.0)
        else:
            x = x + jnp.where(lane >= sh, pltpu.roll(x, sh, 1), 0.0)
        sh *= 2
    return x


def _mlstm_body(mq_ref, mk_ref, mv_ref, mo_ref, cwq_ref, cwk_ref, cbq_ref, cbk_ref, gates_ref, hg_ref,
                o_ref, q_s, kt_s, gsc, hbuf, c_s):
    seq, dh = mq_ref.shape
    L = MLSTM_CHUNK
    nc = seq // L
    halo = BF16_ROWS

    def conv_chunk(src_ref, w_ref, b_ref, c):
        start = pl.multiple_of(c * L, L)
        cur = src_ref[pl.ds(start, L), :].astype(F32)
        prev_start = pl.multiple_of(jnp.maximum(start - halo, 0), halo)
        next_start = pl.multiple_of(jnp.minimum(start + L, seq - halo), halo)
        prev = src_ref[pl.ds(prev_start, halo), :].astype(F32) * jnp.where(c > 0, 1.0, 0.0)
        nxt = src_ref[pl.ds(next_start, halo), :].astype(F32) * jnp.where(c < nc - 1, 1.0, 0.0)
        ext = jnp.concatenate([prev, cur, nxt], axis=0)
        w = w_ref[...]
        out = jnp.broadcast_to(b_ref[...], (L, dh))
        pad = CONV_WIDTH // 2
        for t in range(CONV_WIDTH):
            lo = halo + t - pad
            out = out + ext[lo:lo + L, :] * w[t:t + 1, :]
        return out * jax.nn.sigmoid(out)

    def prep(c, carry):
        start = pl.multiple_of(c * L, L)
        qc = conv_chunk(mq_ref, cwq_ref, cbq_ref, c) * (MLSTM_HEAD_DIM ** -0.5)
        q_s[pl.ds(start, L), :] = qc.astype(BF16)
        kc = conv_chunk(mk_ref, cwk_ref, cbk_ref, c)
        kt_s[c] = kc.T.astype(BF16)
        return carry

    lax.fori_loop(0, nc, prep, 0)

    g = gates_ref[...]
    b_f = _lane_scan(_log_sigmoid(g[2]), reverse=False)
    b_b = _lane_scan(_log_sigmoid(g[3]), reverse=True)
    gsc[0] = b_f
    gsc[1] = g[0] - b_f
    gsc[2] = b_b
    gsc[3] = g[1] - b_b

    c_s[...] = jnp.zeros(c_s.shape, F32)

    row_i = lax.broadcasted_iota(jnp.int32, (L, L), 0)
    col_i = lax.broadcasted_iota(jnp.int32, (L, L), 1)
    lane_row = lax.broadcasted_iota(jnp.int32, (1, L), 1)
    ones_col = jnp.where(lax.broadcasted_iota(jnp.int32, (L, LANES), 1) == 0, 1.0, 0.0).astype(BF16)

    def chunk_of(k, direction):
        return k if direction == 0 else nc - 1 - k

    def chunk_step(k, m, direction):
        c = chunk_of(k, direction)
        start = pl.multiple_of(c * L, L)
        causal = (col_i <= row_i) if direction == 0 else (col_i >= row_i)
        last_lane = L - 1 if direction == 0 else 0
        q = q_s[pl.ds(start, L), :]
        kt = kt_s[c]
        vaug = jnp.concatenate([mv_ref[pl.ds(start, L), :], ones_col], axis=1)
        brow = gsc[2 * direction, pl.ds(c, 1), :]
        rrow = gsc[2 * direction + 1, pl.ds(c, 1), :]
        bcol = jnp.sum(jnp.where(row_i == col_i, brow, 0.0), axis=1, keepdims=True)
        dmat = jnp.where(causal, bcol + rrow, -jnp.inf)
        dmax = jnp.max(dmat, axis=1, keepdims=True)
        c_old = c_s[direction]
        q_out = jnp.dot(q, jnp.concatenate([kt, c_old.astype(BF16)], axis=1),
                        preferred_element_type=F32)
        qk, p_inter = q_out[:, :L], q_out[:, L:]
        st = (qk * jnp.exp(dmat - dmax)).astype(BF16)
        b_last = jnp.sum(jnp.where(lane_row == last_lane, brow, 0.0), axis=1, keepdims=True)
        grow = b_last + rrow
        gmax = jnp.max(grow, axis=1, keepdims=True)
        wk = jnp.exp(grow - gmax)
        ktw = (kt.astype(F32) * wk).astype(BF16)
        v_out = jnp.dot(jnp.concatenate([st, ktw], axis=0), vaug, preferred_element_type=F32)
        p_intra, upd = v_out[:L], v_out[L:]
        inter = bcol + m
        m_t = jnp.maximum(inter, dmax)
        num_aug = jnp.exp(inter - m_t) * p_inter + jnp.exp(dmax - m_t) * p_intra
        den = num_aug[:, dh:dh + 1]
        hout = num_aug[:, :dh] / jnp.maximum(jnp.abs(den), jnp.exp(-m_t))
        m_new = jnp.maximum(b_last + m, gmax)
        c_s[direction] = jnp.exp(b_last + m - m_new) * c_old + jnp.exp(gmax - m_new) * upd
        return hout, m_new

    def finalize(c, hm):
        start = pl.multiple_of(c * L, L)
        y = _rms(hm, hg_ref[...])
        og = jax.nn.sigmoid(mo_ref[pl.ds(start, L), :].astype(F32))
        o_ref[pl.ds(start, L), :] = (og * y).astype(BF16)

    def run_half(first_step, ms, second_half):
        def body(k, ms):
            out = []
            for direction in range(2):
                hout, m_new = chunk_step(k, ms[direction], direction)
                rows = pl.ds(pl.multiple_of(chunk_of(k, direction) * L, L), L)
                if second_half:
                    finalize(chunk_of(k, direction), hout + hbuf[rows, :])
                else:
                    hbuf[rows, :] = hout
                out.append(m_new)
            return tuple(out)

        return lax.fori_loop(first_step, first_step + nc // 2, body, ms, unroll=2)

    m0 = jnp.zeros((1, 1), F32)
    ms = run_half(0, (m0, m0), False)
    run_half(nc // 2, ms, True)


def _mlstm(proj, gates, conv_w, conv_b, head_g):
    nb, seq, _ = proj.shape
    dh = MLSTM_HEAD_DIM
    nc = seq // MLSTM_CHUNK
    assert seq % MLSTM_CHUNK == 0 and nc % 2 == 0
    base = 3 * ATT_WIDTH // dh
    nh = MLSTM_HEADS

    def col(group):
        return pl.BlockSpec((None, seq, dh), lambda b, h: (b, 0, base + group * nh + h))

    return pl.pallas_call(
        _mlstm_body,
        out_shape=jax.ShapeDtypeStruct((nb, seq, MLSTM_WIDTH), BF16),
        grid=(nb, nh),
        in_specs=[
            col(0), col(1), col(2), col(3),
            pl.BlockSpec((CONV_WIDTH, dh), lambda b, h: (0, h)),
            pl.BlockSpec((CONV_WIDTH, dh), lambda b, h: (0, nh + h)),
            pl.BlockSpec((1, dh), lambda b, h: (0, h)),
            pl.BlockSpec((1, dh), lambda b, h: (0, nh + h)),
            pl.BlockSpec((4, None, None, nc, MLSTM_CHUNK), lambda b, h: (0, h, b, 0, 0)),
            pl.BlockSpec((1, dh), lambda b, h: (0, h)),
        ],
        out_specs=pl.BlockSpec((None, seq, dh), lambda b, h: (b, 0, h)),
        scratch_shapes=[
            pltpu.VMEM((seq, dh), BF16),
            pltpu.VMEM((nc, dh, MLSTM_CHUNK), BF16),
            pltpu.VMEM((4, nc, MLSTM_CHUNK), F32),
            pltpu.VMEM((seq, dh), F32),
            pltpu.VMEM((2, dh, dh + LANES), F32),
        ],
        compiler_params=_params(2),
        name="mlstm",
    )(proj, proj, proj, proj, conv_w, conv_w, conv_b, conv_b, gates, head_g)


def _outproj_body(att_ref, ml_ref, wa_ref, wm_ref, x_ref, g_ref, o_ref):
    mixed = (jnp.dot(att_ref[...], wa_ref[...], preferred_element_type=F32)
             + jnp.dot(ml_ref[...], wm_ref[...], preferred_element_type=F32))
    o_ref[...] = x_ref[...] + _rms(mixed, g_ref[...])


def _out_proj(att, ml, w_out, x, g, *, tm):
    t, d = x.shape
    wa = att.shape[1]
    wm = ml.shape[1]
    assert wa == wm
    return pl.pallas_call(
        _outproj_body,
        out_shape=jax.ShapeDtypeStruct((t, d), F32),
        grid=(t // tm,),
        in_specs=[
            pl.BlockSpec((tm, wa), lambda i: (i, 0)),
            pl.BlockSpec((tm, wm), lambda i: (i, 0)),
            pl.BlockSpec((wa, d), lambda i: (0, 0)),
            pl.BlockSpec((wm, d), lambda i: (1, 0)),
            pl.BlockSpec((tm, d), lambda i: (i, 0)),
            pl.BlockSpec((1, d), lambda i: (0, 0)),
        ],
        out_specs=pl.BlockSpec((tm, d), lambda i: (i, 0)),
        compiler_params=_params(1),
        name="out_proj",
    )(att, ml, w_out, w_out, x, g)


def _layer(xs, rel_bias, ffn1_pre_g, ffn1_post_g, ffn1_w_gate, ffn1_w_up, ffn1_w_down,
           mix_pre_g, mix_post_g, w_in, gate_bias, conv_w, conv_b, lam_vecs,
           att_head_g, mlstm_head_g, w_out,
           ffn2_pre_g, ffn2_post_g, ffn2_w_gate, ffn2_w_up, ffn2_w_down,
           *, tm, tm_proj, tf, tn, attn_tile):
    seq, d = xs[0].shape[1:]
    rows = tuple(x.shape[0] * seq for x in xs)
    t = sum(rows)
    nb = t // seq
    row = lambda v: v.reshape(1, -1).astype(F32)

    (x1,) = _ffn(tuple(x.reshape(-1, d) for x in xs), row(ffn1_pre_g), row(ffn1_post_g),
                 ffn1_w_gate.astype(BF16), ffn1_w_up.astype(BF16), ffn1_w_down.astype(BF16), (t,),
                 tm=tm, tf=tf)

    w_main = w_in.astype(BF16)
    w_gate = jnp.pad(w_main[:, MAIN_COLS:], ((0, 0), (0, LANES - N_GATES)))
    gbias = jnp.pad(row(gate_bias), ((0, 0), (0, LANES - N_GATES)))
    col_scale = jnp.concatenate([jnp.full((1, ATT_WIDTH), ATT_QK_DIM ** -0.5 * LOG2E, F32),
                                 jnp.ones((1, MAIN_COLS - ATT_WIDTH), F32)], axis=1)
    proj, gates = _in_proj(x1, row(mix_pre_g), w_main, col_scale, w_gate, gbias, tm=tm_proj, tn=tn)
    proj = proj.reshape(nb, seq, MAIN_COLS)
    nc = seq // MLSTM_CHUNK
    gates = gates[:N_GATES].reshape(N_GATES // MLSTM_HEADS, MLSTM_HEADS, nb, nc, MLSTM_CHUNK)

    btiles = _bias_tiles(rel_bias.astype(F32))
    att = _attention(proj, btiles, rel_bias.astype(F32), lam_vecs, row(att_head_g), tile=attn_tile)
    ml = _mlstm(proj, gates, conv_w.astype(F32), row(conv_b), row(mlstm_head_g))

    x2 = _out_proj(att.reshape(t, ATT_WIDTH), ml.reshape(t, MLSTM_WIDTH), w_out.astype(BF16),
                   x1, row(mix_post_g), tm=tm)
    ys = _ffn((x2,), row(ffn2_pre_g), row(ffn2_post_g), ffn2_w_gate.astype(BF16),
              ffn2_w_up.astype(BF16), ffn2_w_down.astype(BF16), rows, tm=tm, tf=tf)
    return tuple(y.reshape(x.shape) for y, x in zip(ys, xs))


def kernel(x_prompt, x_sample, rel_bias, ffn1_pre_g, ffn1_post_g, ffn1_w_gate, ffn1_w_up, ffn1_w_down, mix_pre_g, mix_post_g, w_in, gate_bias, conv_w, conv_b, lambda_q1, lambda_k1, lambda_q2, lambda_k2, att_head_g, mlstm_head_g, w_out, ffn2_pre_g, ffn2_post_g, ffn2_w_gate, ffn2_w_up, ffn2_w_down, *, tm=512, tm_proj=1024, tf=512, tn=1792, attn_tile=512):
    assert x_prompt.shape[1:] == x_sample.shape[1:]
    assert ffn1_pre_g.shape[0] == 1, "single-layer trunk"
    lam_vecs = jnp.concatenate([lambda_q1, lambda_k1, lambda_q2, lambda_k2], axis=0).astype(F32)
    return _layer((x_prompt, x_sample), rel_bias, ffn1_pre_g[0], ffn1_post_g[0], ffn1_w_gate[0], ffn1_w_up[0], ffn1_w_down[0],
               mix_pre_g[0], mix_post_g[0], w_in[0], gate_bias[0], conv_w[0], conv_b[0], lam_vecs,
               att_head_g[0], mlstm_head_g[0], w_out[0],
               ffn2_pre_g[0], ffn2_post_g[0], ffn2_w_gate[0], ffn2_w_up[0], ffn2_w_down[0],
               tm=tm, tm_proj=tm_proj, tf=tf, tn=tn, attn_tile=attn_tile)
```

```python
import functools
import math

import jax
import jax.numpy as jnp
from jax import lax
from jax.experimental import pallas as pl
from jax.experimental.pallas import tpu as pltpu

F32 = jnp.float32
BF16 = jnp.bfloat16

EPS = 1e-6
ATT_HEADS = 8
ATT_QK_DIM = 64
ATT_V_DIM = 128
ATT_WIDTH = ATT_HEADS * ATT_V_DIM
MLSTM_HEADS = 4
MLSTM_HEAD_DIM = 256
MLSTM_WIDTH = MLSTM_HEADS * MLSTM_HEAD_DIM
MLSTM_CHUNK = 128
CONV_WIDTH = 5
N_GATES = 4 * MLSTM_HEADS
MAIN_COLS = 3 * ATT_WIDTH + 4 * MLSTM_WIDTH
REL_BUCKETS = 32
REL_MAX_DIST = 128
LAMBDA_INIT = 0.8 - 0.6 * math.exp(-0.3 * 0)
LOG2E = math.log2(math.e)

LANES = 128
BF16_ROWS = 16
VMEM_LIMIT = 56 * 1024 * 1024
assert REL_MAX_DIST <= LANES


def _params(n_axes):
    return pltpu.CompilerParams(dimension_semantics=("arbitrary",) * n_axes,
                                vmem_limit_bytes=VMEM_LIMIT)


def _rms(xf, g_row):
    ms = jnp.mean(xf * xf, axis=-1, keepdims=True)
    return xf * lax.rsqrt(ms + EPS) * g_row


def _segment_of(i, tile_starts):
    seg = 0
    for start in tile_starts[1:]:
        seg = seg + (i >= start).astype(jnp.int32)
    return seg


def _ffn_body(*refs, in_starts, out_starts, tf):
    n_in, n_out = len(in_starts), len(out_starts)
    x_refs = refs[:n_in]
    pre_g_ref, post_g_ref, wg_hbm, wu_hbm, wd_hbm = refs[n_in:n_in + 5]
    o_refs = refs[n_in + 5:n_in + 5 + n_out]
    h_ref, acc_ref, wg_buf, wu_buf, wd_buf, sem = refs[n_in + 5 + n_out:]
    n_chunks = wg_hbm.shape[1] // tf
    i = pl.program_id(0)
    n_tiles = pl.num_programs(0)
    in_seg = _segment_of(i, in_starts)
    out_seg = _segment_of(i, out_starts)

    def chunk_copies(f, slot):
        cols = pl.ds(f * tf, tf)
        return (pltpu.make_async_copy(wg_hbm.at[:, cols], wg_buf.at[slot], sem.at[0, slot]),
                pltpu.make_async_copy(wu_hbm.at[:, cols], wu_buf.at[slot], sem.at[1, slot]),
                pltpu.make_async_copy(wd_hbm.at[cols, :], wd_buf.at[slot], sem.at[2, slot]))

    first_slot = (i * n_chunks) % 2

    @pl.when(i == 0)
    def _():
        for cp in chunk_copies(0, 0):
            cp.start()

    for k, x_ref in enumerate(x_refs):
        @pl.when(in_seg == k)
        def _(x_ref=x_ref):
            h_ref[...] = _rms(x_ref[...], pre_g_ref[...]).astype(BF16)

    h = h_ref[...]
    for f in range(n_chunks):
        slot = (first_slot + f) % 2
        for cp in chunk_copies((f + 1) % n_chunks, 1 - slot):
            cp.start()
        for cp in chunk_copies(f, slot):
            cp.wait()
        g = jnp.dot(h, wg_buf[slot], preferred_element_type=F32)
        u = jnp.dot(h, wu_buf[slot], preferred_element_type=F32)
        a = (g * jax.nn.sigmoid(g) * u).astype(BF16)
        part = jnp.dot(a, wd_buf[slot], preferred_element_type=F32)
        if f == 0:
            acc_ref[...] = part
        else:
            acc_ref[...] += part

    for k, x_ref in enumerate(x_refs):
        for m, o_ref in enumerate(o_refs):
            @pl.when(jnp.logical_and(in_seg == k, out_seg == m))
            def _(x_ref=x_ref, o_ref=o_ref):
                o_ref[...] = x_ref[...] + _rms(acc_ref[...], 0.5 * post_g_ref[...])

    @pl.when(i == n_tiles - 1)
    def _():
        for cp in chunk_copies(0, (first_slot + n_chunks) % 2):
            cp.wait()


def _ffn(xs, pre_g, post_g, wg, wu, wd, out_rows, *, tm, tf):
    d = xs[0].shape[1]
    fdim = wg.shape[1]
    assert fdim % tf == 0

    def tile_starts(rows):
        assert all(r % tm == 0 for r in rows)
        starts = [0]
        for r in rows[:-1]:
            starts.append(starts[-1] + r // tm)
        return tuple(starts)

    in_rows = tuple(x.shape[0] for x in xs)
    assert sum(in_rows) == sum(out_rows)
    in_starts, out_starts = tile_starts(in_rows), tile_starts(out_rows)

    def seg_spec(start, rows):
        n = rows // tm
        return pl.BlockSpec((tm, d), lambda i: (jnp.clip(i - start, 0, n - 1), 0))

    fixed = lambda i: (0, 0)
    hbm = pl.BlockSpec(memory_space=pl.ANY)
    return pl.pallas_call(
        functools.partial(_ffn_body, in_starts=in_starts, out_starts=out_starts, tf=tf),
        out_shape=tuple(jax.ShapeDtypeStruct((r, d), F32) for r in out_rows),
        grid=(sum(in_rows) // tm,),
        in_specs=[seg_spec(s, r) for s, r in zip(in_starts, in_rows)] + [
            pl.BlockSpec((1, d), fixed),
            pl.BlockSpec((1, d), fixed),
            hbm, hbm, hbm,
        ],
        out_specs=tuple(seg_spec(s, r) for s, r in zip(out_starts, out_rows)),
        scratch_shapes=[
            pltpu.VMEM((tm, d), BF16),
            pltpu.VMEM((tm, d), F32),
            pltpu.VMEM((2, d, tf), BF16),
            pltpu.VMEM((2, d, tf), BF16),
            pltpu.VMEM((2, tf, d), BF16),
            pltpu.SemaphoreType.DMA((3, 2)),
        ],
        compiler_params=_params(1),
        name="ffn",
    )(*xs, pre_g, post_g, wg, wu, wd)


def _inproj_body(x_ref, g_ref, w_ref, cs_ref, wgate_ref, gbias_ref, o_ref, gates_ref, h_ref):
    j = pl.program_id(1)

    @pl.when(j == 0)
    def _():
        h = _rms(x_ref[...], g_ref[...]).astype(BF16)
        h_ref[...] = h
        gates = jnp.dot(h, wgate_ref[...], preferred_element_type=F32) + gbias_ref[...]
        gates_ref[...] = gates.T

    o_ref[...] = (jnp.dot(h_ref[...], w_ref[...], preferred_element_type=F32) * cs_ref[...]).astype(BF16)


def _in_proj(x, g, w_main, col_scale, w_gate, gate_bias, *, tm, tn):
    t, d = x.shape
    n = col_scale.shape[1]
    assert n % tn == 0 and n <= w_main.shape[1]
    return pl.pallas_call(
        _inproj_body,
        out_shape=(jax.ShapeDtypeStruct((t, n), BF16), jax.ShapeDtypeStruct((LANES, t), F32)),
        grid=(t // tm, n // tn),
        in_specs=[
            pl.BlockSpec((tm, d), lambda i, j: (i, 0)),
            pl.BlockSpec((1, d), lambda i, j: (0, 0)),
            pl.BlockSpec((d, tn), lambda i, j: (0, j)),
            pl.BlockSpec((1, tn), lambda i, j: (0, j)),
            pl.BlockSpec((d, LANES), lambda i, j: (0, 0)),
            pl.BlockSpec((1, LANES), lambda i, j: (0, 0)),
        ],
        out_specs=(pl.BlockSpec((tm, tn), lambda i, j: (i, j)),
                   pl.BlockSpec((LANES, tm), lambda i, j: (0, i))),
        scratch_shapes=[pltpu.VMEM((tm, d), BF16)],
        compiler_params=_params(2),
        name="in_proj",
    )(x, g, w_main, col_scale, w_gate, gate_bias)


def _rel_bucket(rel):
    nb = REL_BUCKETS // 2
    max_exact = nb // 2
    ret = jnp.where(rel > 0, nb, 0)
    n = jnp.abs(rel)
    nf = jnp.maximum(n, 1).astype(jnp.float32)
    large = max_exact + (jnp.log(nf / max_exact) / math.log(REL_MAX_DIST / max_exact)
                         * (nb - max_exact)).astype(jnp.int32)
    large = jnp.minimum(large, nb - 1)
    return ret + jnp.where(n < max_exact, n, large)


def _bias_tiles_body(relb_ref, bucket_ref, o_ref):
    hc = pl.program_id(0)
    bucket = bucket_ref[...]
    acc = jnp.zeros(bucket.shape, F32)
    for b in range(REL_BUCKETS):
        acc = jnp.where(bucket == b, relb_ref[b, hc], acc)
    o_ref[...] = acc * LOG2E


def _bias_tiles(rel_bias):
    key = lax.broadcasted_iota(jnp.int32, (3, LANES, LANES), 1)
    query = lax.broadcasted_iota(jnp.int32, (3, LANES, LANES), 2)
    off = (lax.broadcasted_iota(jnp.int32, (3, LANES, LANES), 0) - 1) * LANES
    bucket = _rel_bucket(off + key - query).astype(jnp.int32)
    n_hc = rel_bias.shape[1]
    return pl.pallas_call(
        _bias_tiles_body,
        out_shape=jax.ShapeDtypeStruct((n_hc, 3, LANES, LANES), F32),
        grid=(n_hc,),
        in_specs=[
            pl.BlockSpec(memory_space=pltpu.SMEM),
            pl.BlockSpec((3, LANES, LANES), lambda i: (0, 0, 0)),
        ],
        out_specs=pl.BlockSpec((None, 3, LANES, LANES), lambda i: (i, 0, 0, 0)),
        compiler_params=_params(1),
        name="rel_bias_tiles",
    )(rel_bias, bucket)


V_ROWS = ATT_V_DIM + BF16_ROWS


def _attn_body(relb_ref, lam_ref, q_ref, k_ref, v_ref, btile_ref, hg_ref, o_ref,
               qz_ref, vt_ref, bias_ref, sa_ref, sb_ref, ma_ref, mb_ref, ha_ref, hb_ref, acc_ref, m_ref, *, tile):
    h = pl.program_id(0)
    seq = k_ref.shape[0]
    n_tiles = seq // tile
    n_sub = tile // LANES
    far_bucket_neg = REL_BUCKETS // 2 - 1
    far_bucket_pos = REL_BUCKETS - 1

    q = q_ref[...]
    lane = lax.broadcasted_iota(jnp.int32, q.shape, 1)
    zero = jnp.zeros_like(q)
    qz_ref[0] = jnp.where(lane < ATT_QK_DIM, q, zero)
    qz_ref[1] = jnp.where(lane >= ATT_QK_DIM, q, zero)
    ones_rows = jnp.where(lax.broadcasted_iota(jnp.int32, (V_ROWS - ATT_V_DIM, tile), 0) == 0,
                          1.0, 0.0).astype(BF16)
    for jt in range(n_tiles):
        vt_ref[jt, :ATT_V_DIM, :] = v_ref[jt * tile:(jt + 1) * tile, :].T
        vt_ref[jt, ATT_V_DIM:, :] = ones_rows
    @pl.when(pl.program_id(1) == 0)
    def _():
        for c in range(2):
            cneg = relb_ref[far_bucket_neg, 2 * h + c] * LOG2E
            cpos = relb_ref[far_bucket_pos, 2 * h + c] * LOG2E
            for di in range(3):
                for kj in range(n_sub):
                    for qi in range(n_sub):
                        d = (di - 1) * n_sub + kj - qi
                        if -1 <= d <= 1:
                            sub = btile_ref[c, d + 1]
                        else:
                            sub = jnp.full((LANES, LANES), cneg if d < 0 else cpos, F32)
                        bias_ref[c, di, kj * LANES:(kj + 1) * LANES, qi * LANES:(qi + 1) * LANES] = sub

    m_ref[...] = jnp.full(m_ref.shape, -jnp.inf, F32)
    acc_ref[...] = jnp.zeros(acc_ref.shape, F32)

    lam_v = lam_ref[...]
    s1 = jnp.sum(lam_v[0:1] * lam_v[1:2], axis=-1, keepdims=True)
    s2 = jnp.sum(lam_v[2:3] * lam_v[3:4], axis=-1, keepdims=True)
    lam = jnp.exp(s1) - jnp.exp(s2) + LAMBDA_INIT

    def scores(qi, j, bufs, near):
        s_ref, smax_ref, shift_ref = bufs
        kt = k_ref[pl.ds(pl.multiple_of(j * tile, tile), tile), :]
        for c in range(2):
            qz = qz_ref[c, pl.ds(pl.multiple_of(qi * tile, tile), tile), :]
            s = lax.dot_general(kt, qz, (((1,), (1,)), ((), ())), preferred_element_type=F32)
            if near:
                s = s + bias_ref[c, j - qi + 1]
                shift = jnp.zeros((1, tile), F32)
            else:
                shift = jnp.full((1, tile), jnp.where(j < qi, relb_ref[far_bucket_neg, 2 * h + c],
                                                      relb_ref[far_bucket_pos, 2 * h + c]) * LOG2E, F32)
            s_ref[c] = s
            smax_ref[c] = jnp.max(s, axis=0, keepdims=True) + shift
            shift_ref[c] = shift

    def accumulate(j, bufs):
        s_ref, smax_ref, shift_ref = bufs
        vt = vt_ref[j]
        for c in range(2):
            m_old = m_ref[c]
            m_new = jnp.maximum(m_old, smax_ref[c])
            alpha = jnp.exp2(m_old - m_new)
            e = jnp.exp2(s_ref[c] - (m_new - shift_ref[c])).astype(BF16)
            acc_ref[c] = alpha * acc_ref[c] + jnp.dot(vt, e, preferred_element_type=F32)
            m_ref[c] = m_new

    def finalize(qi):
        outs = []
        for c in range(2):
            acc = acc_ref[c]
            outs.append(acc[:ATT_V_DIM] / acc[ATT_V_DIM:ATT_V_DIM + 1])
        o_t = outs[0] - lam * outs[1]
        y_t = o_t * lax.rsqrt(jnp.mean(o_t * o_t, axis=0, keepdims=True) + EPS)
        o_ref[pl.ds(pl.multiple_of(qi * tile, tile), tile), :] = (
            y_t.T * hg_ref[...] * (1.0 - LAMBDA_INIT)).astype(BF16)
        m_ref[...] = jnp.full(m_ref.shape, -jnp.inf, F32)
        acc_ref[...] = jnp.zeros(acc_ref.shape, F32)

    buf_a = (sa_ref, ma_ref, ha_ref)
    buf_b = (sb_ref, mb_ref, hb_ref)

    def is_near(qi, j):
        return jnp.abs(j - qi) <= 1

    def pair(q1, j1, q2, j2, jc1, jc2, tail=None):
        n1, n2 = is_near(q1, j1), is_near(q2, j2)
        for v1 in (True, False):
            for v2 in (True, False):
                @pl.when(jnp.logical_and(n1 == v1, n2 == v2))
                def _(v1=v1, v2=v2):
                    scores(q1, j1, buf_b, v1)
                    accumulate(jc1, buf_a)
                    scores(q2, j2, buf_a, v2)
                    accumulate(jc2, buf_b)
                    if tail is not None:
                        tail()

    scores(0, 0, buf_a, True)

    def q_body(qi, carry):
        def pair_body(t, c2):
            j = 2 * t
            pair(qi, j + 1, qi, j + 2, j, j + 1)
            return c2

        lax.fori_loop(0, n_tiles // 2 - 1, pair_body, 0)
        pair(qi, n_tiles - 1, jnp.minimum(qi + 1, n_tiles - 1), 0, n_tiles - 2, n_tiles - 1,
             tail=lambda: finalize(qi))
        return carry

    lax.fori_loop(0, n_tiles, q_body, 0)


def _attention(proj, btiles, rel_bias, lam_vecs, head_g, *, tile):
    nb, seq, _ = proj.shape
    assert seq % (2 * tile) == 0 and tile % LANES == 0
    btiles = btiles.reshape(ATT_HEADS, 2, 3, LANES, LANES)
    return pl.pallas_call(
        functools.partial(_attn_body, tile=tile),
        out_shape=jax.ShapeDtypeStruct((nb, seq, ATT_WIDTH), BF16),
        grid=(ATT_HEADS, nb),
        in_specs=[
            pl.BlockSpec(memory_space=pltpu.SMEM),
            pl.BlockSpec((4, ATT_QK_DIM), lambda h, b: (0, 0)),
            pl.BlockSpec((None, seq, LANES), lambda h, b: (b, 0, h)),
            pl.BlockSpec((None, seq, LANES), lambda h, b: (b, 0, ATT_HEADS + h)),
            pl.BlockSpec((None, seq, LANES), lambda h, b: (b, 0, 2 * ATT_HEADS + h)),
            pl.BlockSpec((None, 2, 3, LANES, LANES), lambda h, b: (h, 0, 0, 0, 0)),
            pl.BlockSpec((1, ATT_V_DIM), lambda h, b: (0, h)),
        ],
        out_specs=pl.BlockSpec((None, seq, ATT_V_DIM), lambda h, b: (b, 0, h)),
        scratch_shapes=[
            pltpu.VMEM((2, seq, LANES), BF16),
            pltpu.VMEM((seq // tile, V_ROWS, tile), BF16),
            pltpu.VMEM((2, 3, tile, tile), F32),
            pltpu.VMEM((2, tile, tile), F32),
            pltpu.VMEM((2, tile, tile), F32),
            pltpu.VMEM((2, 1, tile), F32),
            pltpu.VMEM((2, 1, tile), F32),
            pltpu.VMEM((2, 1, tile), F32),
            pltpu.VMEM((2, 1, tile), F32),
            pltpu.VMEM((2, V_ROWS, tile), F32),
            pltpu.VMEM((2, 1, tile), F32),
        ],
        compiler_params=_params(2),
        name="diff_attention",
    )(rel_bias, lam_vecs, proj, proj, proj, btiles, head_g)


def _log_sigmoid(x):
    return jnp.minimum(x, 0.0) - jnp.log1p(jnp.exp(-jnp.abs(x)))


def _lane_scan(x, reverse):
    lane = lax.broadcasted_iota(jnp.int32, x.shape, 1)
    sh = 1
    while sh < LANES:
        if reverse:
            x = x + jnp.where(lane < LANES - sh, pltpu.roll(x, LANES - sh, 1), 0.0)
        else:
            x = x + jnp.where(lane >= sh, pltpu.roll(x, sh, 1), 0.0)
        sh *= 2
    return x


def _mlstm_body(mq_ref, mk_ref, mv_ref, mo_ref, cwq_ref, cwk_ref, cbq_ref, cbk_ref, gates_ref, hg_ref,
                o_ref, q_s, kt_s, gsc, hbuf, c_s):
    seq, dh = mq_ref.shape
    L = MLSTM_CHUNK
    nc = seq // L
    halo = BF16_ROWS

    def conv_chunk(src_ref, w_ref, b_ref, c):
        start = pl.multiple_of(c * L, L)
        cur = src_ref[pl.ds(start, L), :].astype(F32)
        prev_start = pl.multiple_of(jnp.maximum(start - halo, 0), halo)
        next_start = pl.multiple_of(jnp.minimum(start + L, seq - halo), halo)
        prev = src_ref[pl.ds(prev_start, halo), :].astype(F32) * jnp.where(c > 0, 1.0, 0.0)
        nxt = src_ref[pl.ds(next_start, halo), :].astype(F32) * jnp.where(c < nc - 1, 1.0, 0.0)
        ext = jnp.concatenate([prev, cur, nxt], axis=0)
        w = w_ref[...]
        out = jnp.broadcast_to(b_ref[...], (L, dh))
        pad = CONV_WIDTH // 2
        for t in range(CONV_WIDTH):
            lo = halo + t - pad
            out = out + ext[lo:lo + L, :] * w[t:t + 1, :]
        return out * jax.nn.sigmoid(out)

    def prep(c, carry):
        start = pl.multiple_of(c * L, L)
        qc = conv_chunk(mq_ref, cwq_ref, cbq_ref, c) * (MLSTM_HEAD_DIM ** -0.5)
        q_s[pl.ds(start, L), :] = qc.astype(BF16)
        kc = conv_chunk(mk_ref, cwk_ref, cbk_ref, c)
        kt_s[c] = kc.T.astype(BF16)
        return carry

    lax.fori_loop(0, nc, prep, 0)

    g = gates_ref[...]
    b_f = _lane_scan(_log_sigmoid(g[2]), reverse=False)
    b_b = _lane_scan(_log_sigmoid(g[3]), reverse=True)
    gsc[0] = b_f
    gsc[1] = g[0] - b_f
    gsc[2] = b_b
    gsc[3] = g[1] - b_b

    c_s[...] = jnp.zeros(c_s.shape, F32)

    row_i = lax.broadcasted_iota(jnp.int32, (L, L), 0)
    col_i = lax.broadcasted_iota(jnp.int32, (L, L), 1)
    lane_row = lax.broadcasted_iota(jnp.int32, (1, L), 1)
    ones_col = jnp.where(lax.broadcasted_iota(jnp.int32, (L, LANES), 1) == 0, 1.0, 0.0).astype(BF16)

    def chunk_of(k, direction):
        return k if direction == 0 else nc - 1 - k

    def chunk_step(k, m, direction):
        c = chunk_of(k, direction)
        start = pl.multiple_of(c * L, L)
        causal = (col_i <= row_i) if direction == 0 else (col_i >= row_i)
        last_lane = L - 1 if direction == 0 else 0
        q = q_s[pl.ds(start, L), :]
        kt = kt_s[c]
        vaug = jnp.concatenate([mv_ref[pl.ds(start, L), :], ones_col], axis=1)
        brow = gsc[2 * direction, pl.ds(c, 1), :]
        rrow = gsc[2 * direction + 1, pl.ds(c, 1), :]
        bcol = jnp.sum(jnp.where(row_i == col_i, brow, 0.0), axis=1, keepdims=True)
        dmat = jnp.where(causal, bcol + rrow, -jnp.inf)
        dmax = jnp.max(dmat, axis=1, keepdims=True)
        c_old = c_s[direction]
        q_out = jnp.dot(q, jnp.concatenate([kt, c_old.astype(BF16)], axis=1),
                        preferred_element_type=F32)
        qk, p_inter = q_out[:, :L], q_out[:, L:]
        st = (qk * jnp.exp(dmat - dmax)).astype(BF16)
        b_last = jnp.sum(jnp.where(lane_row == last_lane, brow, 0.0), axis=1, keepdims=True)
        grow = b_last + rrow
        gmax = jnp.max(grow, axis=1, keepdims=True)
        wk = jnp.exp(grow - gmax)
        ktw = (kt.astype(F32) * wk).astype(BF16)
        v_out = jnp.dot(jnp.concatenate([st, ktw], axis=0), vaug, preferred_element_type=F32)
        p_intra, upd = v_out[:L], v_out[L:]
        inter = bcol + m
        m_t = jnp.maximum(inter, dmax)
        num_aug = jnp.exp(inter - m_t) * p_inter + jnp.exp(dmax - m_t) * p_intra
        den = num_aug[:, dh:dh + 1]
        hout = num_aug[:, :dh] / jnp.maximum(jnp.abs(den), jnp.exp(-m_t))
        m_new = jnp.maximum(b_last + m, gmax)
        c_s[direction] = jnp.exp(b_last + m - m_new) * c_old + jnp.exp(gmax - m_new) * upd
        return hout, m_new

    def finalize(c, hm):
        start = pl.multiple_of(c * L, L)
        y = _rms(hm, hg_ref[...])
        og = jax.nn.sigmoid(mo_ref[pl.ds(start, L), :].astype(F32))
        o_ref[pl.ds(start, L), :] = (og * y).astype(BF16)

    def run_half(first_step, ms, second_half):
        def body(k, ms):
            out = []
            for direction in range(2):
                hout, m_new = chunk_step(k, ms[direction], direction)
                rows = pl.ds(pl.multiple_of(chunk_of(k, direction) * L, L), L)
                if second_half:
                    finalize(chunk_of(k, direction), hout + hbuf[rows, :])
                else:
                    hbuf[rows, :] = hout
                out.append(m_new)
            return tuple(out)

        return lax.fori_loop(first_step, first_step + nc // 2, body, ms, unroll=2)

    m0 = jnp.zeros((1, 1), F32)
    ms = run_half(0, (m0, m0), False)
    run_half(nc // 2, ms, True)


def _mlstm(proj, gates, conv_w, conv_b, head_g):
    nb, seq, _ = proj.shape
    dh = MLSTM_HEAD_DIM
    nc = seq // MLSTM_CHUNK
    assert seq % MLSTM_CHUNK == 0 and nc % 2 == 0
    base = 3 * ATT_WIDTH // dh
    nh = MLSTM_HEADS

    def col(group):
        return pl.BlockSpec((None, seq, dh), lambda b, h: (b, 0, base + group * nh + h))

    return pl.pallas_call(
        _mlstm_body,
        out_shape=jax.ShapeDtypeStruct((nb, seq, MLSTM_WIDTH), BF16),
        grid=(nb, nh),
        in_specs=[
            col(0), col(1), col(2), col(3),
            pl.BlockSpec((CONV_WIDTH, dh), lambda b, h: (0, h)),
            pl.BlockSpec((CONV_WIDTH, dh), lambda b, h: (0, nh + h)),
            pl.BlockSpec((1, dh), lambda b, h: (0, h)),
            pl.BlockSpec((1, dh), lambda b, h: (0, nh + h)),
            pl.BlockSpec((4, None, None, nc, MLSTM_CHUNK), lambda b, h: (0, h, b, 0, 0)),
            pl.BlockSpec((1, dh), lambda b, h: (0, h)),
        ],
        out_specs=pl.BlockSpec((None, seq, dh), lambda b, h: (b, 0, h)),
        scratch_shapes=[
            pltpu.VMEM((seq, dh), BF16),
            pltpu.VMEM((nc, dh, MLSTM_CHUNK), BF16),
            pltpu.VMEM((4, nc, MLSTM_CHUNK), F32),
            pltpu.VMEM((seq, dh), F32),
            pltpu.VMEM((2, dh, dh + LANES), F32),
        ],
        compiler_params=_params(2),
        name="mlstm",
    )(proj, proj, proj, proj, conv_w, conv_w, conv_b, conv_b, gates, head_g)


def _outproj_body(att_ref, ml_ref, wa_ref, wm_ref, x_ref, g_ref, o_ref):
    mixed = (jnp.dot(att_ref[...], wa_ref[...], preferred_element_type=F32)
             + jnp.dot(ml_ref[...], wm_ref[...], preferred_element_type=F32))
    o_ref[...] = x_ref[...] + _rms(mixed, g_ref[...])


def _out_proj(att, ml, w_out, x, g, *, tm):
    t, d = x.shape
    wa = att.shape[1]
    wm = ml.shape[1]
    assert wa == wm
    return pl.pallas_call(
        _outproj_body,
        out_shape=jax.ShapeDtypeStruct((t, d), F32),
        grid=(t // tm,),
        in_specs=[
            pl.BlockSpec((tm, wa), lambda i: (i, 0)),
            pl.BlockSpec((tm, wm), lambda i: (i, 0)),
            pl.BlockSpec((wa, d), lambda i: (0, 0)),
            pl.BlockSpec((wm, d), lambda i: (1, 0)),
            pl.BlockSpec((tm, d), lambda i: (i, 0)),
            pl.BlockSpec((1, d), lambda i: (0, 0)),
        ],
        out_specs=pl.BlockSpec((tm, d), lambda i: (i, 0)),
        compiler_params=_params(1),
        name="out_proj",
    )(att, ml, w_out, w_out, x, g)


def _layer(xs, rel_bias, ffn1_pre_g, ffn1_post_g, ffn1_w_gate, ffn1_w_up, ffn1_w_down,
           mix_pre_g, mix_post_g, w_in, gate_bias, conv_w, conv_b, lam_vecs,
           att_head_g, mlstm_head_g, w_out,
           ffn2_pre_g, ffn2_post_g, ffn2_w_gate, ffn2_w_up, ffn2_w_down,
           *, tm, tm_proj, tf, tn, attn_tile):
    seq, d = xs[0].shape[1:]
    rows = tuple(x.shape[0] * seq for x in xs)
    t = sum(rows)
    nb = t // seq
    row = lambda v: v.reshape(1, -1).astype(F32)

    (x1,) = _ffn(tuple(x.reshape(-1, d) for x in xs), row(ffn1_pre_g), row(ffn1_post_g),
                 ffn1_w_gate.astype(BF16), ffn1_w_up.astype(BF16), ffn1_w_down.astype(BF16), (t,),
                 tm=tm, tf=tf)

    w_main = w_in.astype(BF16)
    w_gate = jnp.pad(w_main[:, MAIN_COLS:], ((0, 0), (0, LANES - N_GATES)))
    gbias = jnp.pad(row(gate_bias), ((0, 0), (0, LANES - N_GATES)))
    col_scale = jnp.concatenate([jnp.full((1, ATT_WIDTH), ATT_QK_DIM ** -0.5 * LOG2E, F32),
                                 jnp.ones((1, MAIN_COLS - ATT_WIDTH), F32)], axis=1)
    proj, gates = _in_proj(x1, row(mix_pre_g), w_main, col_scale, w_gate, gbias, tm=tm_proj, tn=tn)
    proj = proj.reshape(nb, seq, MAIN_COLS)
    nc = seq // MLSTM_CHUNK
    gates = gates[:N_GATES].reshape(N_GATES // MLSTM_HEADS, MLSTM_HEADS, nb, nc, MLSTM_CHUNK)

    btiles = _bias_tiles(rel_bias.astype(F32))
    att = _attention(proj, btiles, rel_bias.astype(F32), lam_vecs, row(att_head_g), tile=attn_tile)
    ml = _mlstm(proj, gates, conv_w.astype(F32), row(conv_b), row(mlstm_head_g))

    x2 = _out_proj(att.reshape(t, ATT_WIDTH), ml.reshape(t, MLSTM_WIDTH), w_out.astype(BF16),
                   x1, row(mix_post_g), tm=tm)
    ys = _ffn((x2,), row(ffn2_pre_g), row(ffn2_post_g), ffn2_w_gate.astype(BF16),
              ffn2_w_up.astype(BF16), ffn2_w_down.astype(BF16), rows, tm=tm, tf=tf)
    return tuple(y.reshape(x.shape) for y, x in zip(ys, xs))


def kernel(x_prompt, x_sample, rel_bias, ffn1_pre_g, ffn1_post_g, ffn1_w_gate, ffn1_w_up, ffn1_w_down, mix_pre_g, mix_post_g, w_in, gate_bias, conv_w, conv_b, lambda_q1, lambda_k1, lambda_q2, lambda_k2, att_head_g, mlstm_head_g, w_out, ffn2_pre_g, ffn2_post_g, ffn2_w_gate, ffn2_w_up, ffn2_w_down, *, tm=512, tm_proj=1024, tf=512, tn=1792, attn_tile=512):
    assert x_prompt.shape[1:] == x_sample.shape[1:]
    assert ffn1_pre_g.shape[0] == 1, "single-layer trunk"
    lam_vecs = jnp.concatenate([lambda_q1, lambda_k1, lambda_q2, lambda_k2], axis=0).astype(F32)
    return _layer((x_prompt, x_sample), rel_bias, ffn1_pre_g[0], ffn1_post_g[0], ffn1_w_gate[0], ffn1_w_up[0], ffn1_w_down[0],
               mix_pre_g[0], mix_post_g[0], w_in[0], gate_bias[0], conv_w[0], conv_b[0], lam_vecs,
               att_head_g[0], mlstm_head_g[0], w_out[0],
               ffn2_pre_g[0], ffn2_post_g[0], ffn2_w_gate[0], ffn2_w_up[0], ffn2_w_down[0],
               tm=tm, tm_proj=tm_proj, tf=tf, tn=tn, attn_tile=attn_tile)
```

```python
import functools
import math

import jax
import jax.numpy as jnp
from jax import lax
from jax.experimental import pallas as pl
from jax.experimental.pallas import tpu as pltpu

F32 = jnp.float32
BF16 = jnp.bfloat16

EPS = 1e-6
ATT_HEADS = 8
ATT_QK_DIM = 64
ATT_V_DIM = 128
ATT_WIDTH = ATT_HEADS * ATT_V_DIM
MLSTM_HEADS = 4
MLSTM_HEAD_DIM = 256
MLSTM_WIDTH = MLSTM_HEADS * MLSTM_HEAD_DIM
MLSTM_CHUNK = 128
CONV_WIDTH = 5
N_GATES = 4 * MLSTM_HEADS
MAIN_COLS = 3 * ATT_WIDTH + 4 * MLSTM_WIDTH
REL_BUCKETS = 32
REL_MAX_DIST = 128
LAMBDA_INIT = 0.8 - 0.6 * math.exp(-0.3 * 0)
LOG2E = math.log2(math.e)

LANES = 128
BF16_ROWS = 16
VMEM_LIMIT = 56 * 1024 * 1024
assert REL_MAX_DIST <= LANES


def _params(n_axes):
    return pltpu.CompilerParams(dimension_semantics=("arbitrary",) * n_axes,
                                vmem_limit_bytes=VMEM_LIMIT)


def _rms(xf, g_row):
    ms = jnp.mean(xf * xf, axis=-1, keepdims=True)
    return xf * lax.rsqrt(ms + EPS) * g_row


def _segment_of(i, tile_starts):
    seg = 0
    for start in tile_starts[1:]:
        seg = seg + (i >= start).astype(jnp.int32)
    return seg


def _ffn_body(*refs, in_starts, out_starts, tf):
    n_in, n_out = len(in_starts), len(out_starts)
    x_refs = refs[:n_in]
    pre_g_ref, post_g_ref, wg_hbm, wu_hbm, wd_hbm = refs[n_in:n_in + 5]
    o_refs = refs[n_in + 5:n_in + 5 + n_out]
    h_ref, acc_ref, wg_buf, wu_buf, wd_buf, sem = refs[n_in + 5 + n_out:]
    n_chunks = wg_hbm.shape[0]
    i = pl.program_id(0)
    n_tiles = pl.num_programs(0)
    in_seg = _segment_of(i, in_starts)
    out_seg = _segment_of(i, out_starts)

    def chunk_copies(f, slot):
        return (pltpu.make_async_copy(wg_hbm.at[f], wg_buf.at[slot], sem.at[0, slot]),
                pltpu.make_async_copy(wu_hbm.at[f], wu_buf.at[slot], sem.at[1, slot]),
                pltpu.make_async_copy(wd_hbm.at[f], wd_buf.at[slot], sem.at[2, slot]))

    first_slot = (i * n_chunks) % 2

    @pl.when(i == 0)
    def _():
        for cp in chunk_copies(0, 0):
            cp.start()

    for k, x_ref in enumerate(x_refs):
        @pl.when(in_seg == k)
        def _(x_ref=x_ref):
            h_ref[...] = _rms(x_ref[...], pre_g_ref[...]).astype(BF16)

    h = h_ref[...]
    for f in range(n_chunks):
        slot = (first_slot + f) % 2
        for cp in chunk_copies((f + 1) % n_chunks, 1 - slot):
            cp.start()
        for cp in chunk_copies(f, slot):
            cp.wait()
        g = jnp.dot(h, wg_buf[slot], preferred_element_type=F32)
        u = jnp.dot(h, wu_buf[slot], preferred_element_type=F32)
        a = (g * jax.nn.sigmoid(g) * u).astype(BF16)
        part = jnp.dot(a, wd_buf[slot], preferred_element_type=F32)
        if f == 0:
            acc_ref[...] = part
        else:
            acc_ref[...] += part

    for k, x_ref in enumerate(x_refs):
        for m, o_ref in enumerate(o_refs):
            @pl.when(jnp.logical_and(in_seg == k, out_seg == m))
            def _(x_ref=x_ref, o_ref=o_ref):
                o_ref[...] = x_ref[...] + _rms(acc_ref[...], 0.5 * post_g_ref[...])

    @pl.when(i == n_tiles - 1)
    def _():
        for cp in chunk_copies(0, (first_slot + n_chunks) % 2):
            cp.wait()


def _ffn(xs, pre_g, post_g, wg, wu, wd, out_rows, *, tm, tf):
    d = xs[0].shape[1]
    fdim = wg.shape[1]
    assert fdim % tf == 0
    n_chunks = fdim // tf
    wg = wg.reshape(d, n_chunks, tf).transpose(1, 0, 2)
    wu = wu.reshape(d, n_chunks, tf).transpose(1, 0, 2)
    wd = wd.reshape(n_chunks, tf, d)

    def tile_starts(rows):
        assert all(r % tm == 0 for r in rows)
        starts = [0]
        for r in rows[:-1]:
            starts.append(starts[-1] + r // tm)
        return tuple(starts)

    in_rows = tuple(x.shape[0] for x in xs)
    assert sum(in_rows) == sum(out_rows)
    in_starts, out_starts = tile_starts(in_rows), tile_starts(out_rows)

    def seg_spec(start, rows):
        n = rows // tm
        return pl.BlockSpec((tm, d), lambda i: (jnp.clip(i - start, 0, n - 1), 0))

    fixed = lambda i: (0, 0)
    hbm = pl.BlockSpec(memory_space=pl.ANY)
    return pl.pallas_call(
        functools.partial(_ffn_body, in_starts=in_starts, out_starts=out_starts, tf=tf),
        out_shape=tuple(jax.ShapeDtypeStruct((r, d), F32) for r in out_rows),
        grid=(sum(in_rows) // tm,),
        in_specs=[seg_spec(s, r) for s, r in zip(in_starts, in_rows)] + [
            pl.BlockSpec((1, d), fixed),
            pl.BlockSpec((1, d), fixed),
            hbm, hbm, hbm,
        ],
        out_specs=tuple(seg_spec(s, r) for s, r in zip(out_starts, out_rows)),
        scratch_shapes=[
            pltpu.VMEM((tm, d), BF16),
            pltpu.VMEM((tm, d), F32),
            pltpu.VMEM((2, d, tf), BF16),
            pltpu.VMEM((2, d, tf), BF16),
            pltpu.VMEM((2, tf, d), BF16),
            pltpu.SemaphoreType.DMA((3, 2)),
        ],
        compiler_params=_params(1),
        name="ffn",
    )(*xs, pre_g, post_g, wg, wu, wd)


def _inproj_body(x_ref, g_ref, w_ref, cs_ref, wgate_ref, gbias_ref, o_ref, gates_ref, h_ref):
    j = pl.program_id(1)

    @pl.when(j == 0)
    def _():
        h = _rms(x_ref[...], g_ref[...]).astype(BF16)
        h_ref[...] = h
        gates = jnp.dot(h, wgate_ref[...], preferred_element_type=F32) + gbias_ref[...]
        gates_ref[...] = gates.T

    o_ref[...] = (jnp.dot(h_ref[...], w_ref[...], preferred_element_type=F32) * cs_ref[...]).astype(BF16)


def _in_proj(x, g, w_main, col_scale, w_gate, gate_bias, *, tm, tn):
    t, d = x.shape
    n = col_scale.shape[1]
    assert n % tn == 0 and n <= w_main.shape[1]
    return pl.pallas_call(
        _inproj_body,
        out_shape=(jax.ShapeDtypeStruct((t, n), BF16), jax.ShapeDtypeStruct((LANES, t), F32)),
        grid=(t // tm, n // tn),
        in_specs=[
            pl.BlockSpec((tm, d), lambda i, j: (i, 0)),
            pl.BlockSpec((1, d), lambda i, j: (0, 0)),
            pl.BlockSpec((d, tn), lambda i, j: (0, j)),
            pl.BlockSpec((1, tn), lambda i, j: (0, j)),
            pl.BlockSpec((d, LANES), lambda i, j: (0, 0)),
            pl.BlockSpec((1, LANES), lambda i, j: (0, 0)),
        ],
        out_specs=(pl.BlockSpec((tm, tn), lambda i, j: (i, j)),
                   pl.BlockSpec((LANES, tm), lambda i, j: (0, i))),
        scratch_shapes=[pltpu.VMEM((tm, d), BF16)],
        compiler_params=_params(2),
        name="in_proj",
    )(x, g, w_main, col_scale, w_gate, gate_bias)


def _rel_bucket(rel):
    nb = REL_BUCKETS // 2
    max_exact = nb // 2
    ret = jnp.where(rel > 0, nb, 0)
    n = jnp.abs(rel)
    nf = jnp.maximum(n, 1).astype(jnp.float32)
    large = max_exact + (jnp.log(nf / max_exact) / math.log(REL_MAX_DIST / max_exact)
                         * (nb - max_exact)).astype(jnp.int32)
    large = jnp.minimum(large, nb - 1)
    return ret + jnp.where(n < max_exact, n, large)


def _bias_tiles_body(relb_ref, bucket_ref, o_ref):
    hc = pl.program_id(0)
    bucket = bucket_ref[...]
    acc = jnp.zeros(bucket.shape, F32)
    for b in range(REL_BUCKETS):
        acc = jnp.where(bucket == b, relb_ref[b, hc], acc)
    o_ref[...] = acc * LOG2E


def _bias_tiles(rel_bias):
    key = lax.broadcasted_iota(jnp.int32, (3, LANES, LANES), 1)
    query = lax.broadcasted_iota(jnp.int32, (3, LANES, LANES), 2)
    off = (lax.broadcasted_iota(jnp.int32, (3, LANES, LANES), 0) - 1) * LANES
    bucket = _rel_bucket(off + key - query).astype(jnp.int32)
    n_hc = rel_bias.shape[1]
    return pl.pallas_call(
        _bias_tiles_body,
        out_shape=jax.ShapeDtypeStruct((n_hc, 3, LANES, LANES), F32),
        grid=(n_hc,),
        in_specs=[
            pl.BlockSpec(memory_space=pltpu.SMEM),
            pl.BlockSpec((3, LANES, LANES), lambda i: (0, 0, 0)),
        ],
        out_specs=pl.BlockSpec((None, 3, LANES, LANES), lambda i: (i, 0, 0, 0)),
        compiler_params=_params(1),
        name="rel_bias_tiles",
    )(rel_bias, bucket)


V_ROWS = ATT_V_DIM + BF16_ROWS


def _attn_body(relb_ref, lam_ref, q_ref, k_ref, v_ref, btile_ref, hg_ref, o_ref,
               qz_ref, vt_ref, bias_ref, sa_ref, sb_ref, ma_ref, mb_ref, ha_ref, hb_ref, acc_ref, m_ref, *, tile):
    h = pl.program_id(0)
    seq = k_ref.shape[0]
    n_tiles = seq // tile
    n_sub = tile // LANES
    far_bucket_neg = REL_BUCKETS // 2 - 1
    far_bucket_pos = REL_BUCKETS - 1

    q = q_ref[...]
    lane = lax.broadcasted_iota(jnp.int32, q.shape, 1)
    zero = jnp.zeros_like(q)
    qz_ref[0] = jnp.where(lane < ATT_QK_DIM, q, zero)
    qz_ref[1] = jnp.where(lane >= ATT_QK_DIM, q, zero)
    ones_rows = jnp.where(lax.broadcasted_iota(jnp.int32, (V_ROWS - ATT_V_DIM, tile), 0) == 0,
                          1.0, 0.0).astype(BF16)
    for jt in range(n_tiles):
        vt_ref[jt, :ATT_V_DIM, :] = v_ref[jt * tile:(jt + 1) * tile, :].T
        vt_ref[jt, ATT_V_DIM:, :] = ones_rows
    @pl.when(pl.program_id(1) == 0)
    def _():
        for c in range(2):
            cneg = relb_ref[far_bucket_neg, 2 * h + c] * LOG2E
            cpos = relb_ref[far_bucket_pos, 2 * h + c] * LOG2E
            for di in range(3):
                for kj in range(n_sub):
                    for qi in range(n_sub):
                        d = (di - 1) * n_sub + kj - qi
                        if -1 <= d <= 1:
                            sub = btile_ref[c, d + 1]
                        else:
                            sub = jnp.full((LANES, LANES), cneg if d < 0 else cpos, F32)
                        bias_ref[c, di, kj * LANES:(kj + 1) * LANES, qi * LANES:(qi + 1) * LANES] = sub

    m_ref[...] = jnp.full(m_ref.shape, -jnp.inf, F32)
    acc_ref[...] = jnp.zeros(acc_ref.shape, F32)

    lam_v = lam_ref[...]
    s1 = jnp.sum(lam_v[0:1] * lam_v[1:2], axis=-1, keepdims=True)
    s2 = jnp.sum(lam_v[2:3] * lam_v[3:4], axis=-1, keepdims=True)
    lam = jnp.exp(s1) - jnp.exp(s2) + LAMBDA_INIT

    def scores(qi, j, bufs, near):
        s_ref, smax_ref, shift_ref = bufs
        kt = k_ref[pl.ds(pl.multiple_of(j * tile, tile), tile), :]
        for c in range(2):
            qz = qz_ref[c, pl.ds(pl.multiple_of(qi * tile, tile), tile), :]
            s = lax.dot_general(kt, qz, (((1,), (1,)), ((), ())), preferred_element_type=F32)
            if near:
                s = s + bias_ref[c, j - qi + 1]
                shift = jnp.zeros((1, tile), F32)
            else:
                shift = jnp.full((1, tile), jnp.where(j < qi, relb_ref[far_bucket_neg, 2 * h + c],
                                                      relb_ref[far_bucket_pos, 2 * h + c]) * LOG2E, F32)
            s_ref[c] = s
            smax_ref[c] = jnp.max(s, axis=0, keepdims=True) + shift
            shift_ref[c] = shift

    def accumulate(j, bufs):
        s_ref, smax_ref, shift_ref = bufs
        vt = vt_ref[j]
        for c in range(2):
            m_old = m_ref[c]
            m_new = jnp.maximum(m_old, smax_ref[c])
            alpha = jnp.exp2(m_old - m_new)
            e = jnp.exp2(s_ref[c] - (m_new - shift_ref[c])).astype(BF16)
            acc_ref[c] = alpha * acc_ref[c] + jnp.dot(vt, e, preferred_element_type=F32)
            m_ref[c] = m_new

    def finalize(qi):
        outs = []
        for c in range(2):
            acc = acc_ref[c]
            outs.append(acc[:ATT_V_DIM] / acc[ATT_V_DIM:ATT_V_DIM + 1])
        o_t = outs[0] - lam * outs[1]
        y_t = o_t * lax.rsqrt(jnp.mean(o_t * o_t, axis=0, keepdims=True) + EPS)
        o_ref[pl.ds(pl.multiple_of(qi * tile, tile), tile), :] = (
            y_t.T * hg_ref[...] * (1.0 - LAMBDA_INIT)).astype(BF16)
        m_ref[...] = jnp.full(m_ref.shape, -jnp.inf, F32)
        acc_ref[...] = jnp.zeros(acc_ref.shape, F32)

    buf_a = (sa_ref, ma_ref, ha_ref)
    buf_b = (sb_ref, mb_ref, hb_ref)

    def is_near(qi, j):
        return jnp.abs(j - qi) <= 1

    def pair(q1, j1, q2, j2, jc1, jc2, tail=None):
        n1, n2 = is_near(q1, j1), is_near(q2, j2)
        for v1 in (True, False):
            for v2 in (True, False):
                @pl.when(jnp.logical_and(n1 == v1, n2 == v2))
                def _(v1=v1, v2=v2):
                    scores(q1, j1, buf_b, v1)
                    accumulate(jc1, buf_a)
                    scores(q2, j2, buf_a, v2)
                    accumulate(jc2, buf_b)
                    if tail is not None:
                        tail()

    scores(0, 0, buf_a, True)

    def q_body(qi, carry):
        def pair_body(t, c2):
            j = 2 * t
            pair(qi, j + 1, qi, j + 2, j, j + 1)
            return c2

        lax.fori_loop(0, n_tiles // 2 - 1, pair_body, 0)
        pair(qi, n_tiles - 1, jnp.minimum(qi + 1, n_tiles - 1), 0, n_tiles - 2, n_tiles - 1,
             tail=lambda: finalize(qi))
        return carry

    lax.fori_loop(0, n_tiles, q_body, 0)


def _attention(proj, btiles, rel_bias, lam_vecs, head_g, *, tile):
    nb, seq, _ = proj.shape
    assert seq % (2 * tile) == 0 and tile % LANES == 0
    btiles = btiles.reshape(ATT_HEADS, 2, 3, LANES, LANES)
    return pl.pallas_call(
        functools.partial(_attn_body, tile=tile),
        out_shape=jax.ShapeDtypeStruct((nb, seq, ATT_WIDTH), BF16),
        grid=(ATT_HEADS, nb),
        in_specs=[
            pl.BlockSpec(memory_space=pltpu.SMEM),
            pl.BlockSpec((4, ATT_QK_DIM), lambda h, b: (0, 0)),
            pl.BlockSpec((None, seq, LANES), lambda h, b: (b, 0, h)),
            pl.BlockSpec((None, seq, LANES), lambda h, b: (b, 0, ATT_HEADS + h)),
            pl.BlockSpec((None, seq, LANES), lambda h, b: (b, 0, 2 * ATT_HEADS + h)),
            pl.BlockSpec((None, 2, 3, LANES, LANES), lambda h, b: (h, 0, 0, 0, 0)),
            pl.BlockSpec((1, ATT_V_DIM), lambda h, b: (0, h)),
        ],
        out_specs=pl.BlockSpec((None, seq, ATT_V_DIM), lambda h, b: (b, 0, h)),
        scratch_shapes=[
            pltpu.VMEM((2, seq, LANES), BF16),
            pltpu.VMEM((seq // tile, V_ROWS, tile), BF16),
            pltpu.VMEM((2, 3, tile, tile), F32),
            pltpu.VMEM((2, tile, tile), F32),
            pltpu.VMEM((2, tile, tile), F32),
            pltpu.VMEM((2, 1, tile), F32),
            pltpu.VMEM((2, 1, tile), F32),
            pltpu.VMEM((2, 1, tile), F32),
            pltpu.VMEM((2, 1, tile), F32),
            pltpu.VMEM((2, V_ROWS, tile), F32),
            pltpu.VMEM((2, 1, tile), F32),
        ],
        compiler_params=_params(2),
        name="diff_attention",
    )(rel_bias, lam_vecs, proj, proj, proj, btiles, head_g)


def _log_sigmoid(x):
    return jnp.minimum(x, 0.0) - jnp.log1p(jnp.exp(-jnp.abs(x)))


def _lane_scan(x, reverse):
    lane = lax.broadcasted_iota(jnp.int32, x.shape, 1)
    sh = 1
    while sh < LANES:
        if reverse:
            x = x + jnp.where(lane < LANES - sh, pltpu.roll(x, LANES - sh, 1), 0.0)
        else:
            x = x + jnp.where(lane >= sh, pltpu.roll(x, sh, 1), 0.0)
        sh *= 2
    return x


def _mlstm_body(mq_ref, mk_ref, mv_ref, mo_ref, cwq_ref, cwk_ref, cbq_ref, cbk_ref, gates_ref, hg_ref,
                o_ref, q_s, kt_s, gsc, hbuf, c_s):
    seq, dh = mq_ref.shape
    L = MLSTM_CHUNK
    nc = seq // L
    halo = BF16_ROWS

    def conv_chunk(src_ref, w_ref, b_ref, c):
        start = pl.multiple_of(c * L, L)
        cur = src_ref[pl.ds(start, L), :].astype(F32)
        prev_start = pl.multiple_of(jnp.maximum(start - halo, 0), halo)
        next_start = pl.multiple_of(jnp.minimum(start + L, seq - halo), halo)
        prev = src_ref[pl.ds(prev_start, halo), :].astype(F32) * jnp.where(c > 0, 1.0, 0.0)
        nxt = src_ref[pl.ds(next_start, halo), :].astype(F32) * jnp.where(c < nc - 1, 1.0, 0.0)
        ext = jnp.concatenate([prev, cur, nxt], axis=0)
        w = w_ref[...]
        out = jnp.broadcast_to(b_ref[...], (L, dh))
        pad = CONV_WIDTH // 2
        for t in range(CONV_WIDTH):
            lo = halo + t - pad
            out = out + ext[lo:lo + L, :] * w[t:t + 1, :]
        return out * jax.nn.sigmoid(out)

    def prep(c, carry):
        start = pl.multiple_of(c * L, L)
        qc = conv_chunk(mq_ref, cwq_ref, cbq_ref, c) * (MLSTM_HEAD_DIM ** -0.5)
        q_s[pl.ds(start, L), :] = qc.astype(BF16)
        kc = conv_chunk(mk_ref, cwk_ref, cbk_ref, c)
        kt_s[c] = kc.T.astype(BF16)
        return carry

    lax.fori_loop(0, nc, prep, 0)

    g = gates_ref[...]
    b_f = _lane_scan(_log_sigmoid(g[2]), reverse=False)
    b_b = _lane_scan(_log_sigmoid(g[3]), reverse=True)
    gsc[0] = b_f
    gsc[1] = g[0] - b_f
    gsc[2] = b_b
    gsc[3] = g[1] - b_b

    c_s[...] = jnp.zeros(c_s.shape, F32)

    row_i = lax.broadcasted_iota(jnp.int32, (L, L), 0)
    col_i = lax.broadcasted_iota(jnp.int32, (L, L), 1)
    lane_row = lax.broadcasted_iota(jnp.int32, (1, L), 1)
    ones_col = jnp.where(lax.broadcasted_iota(jnp.int32, (L, LANES), 1) == 0, 1.0, 0.0).astype(BF16)

    def chunk_of(k, direction):
        return k if direction == 0 else nc - 1 - k

    def chunk_step(k, m, direction):
        c = chunk_of(k, direction)
        start = pl.multiple_of(c * L, L)
        causal = (col_i <= row_i) if direction == 0 else (col_i >= row_i)
        last_lane = L - 1 if direction == 0 else 0
        q = q_s[pl.ds(start, L), :]
        kt = kt_s[c]
        vaug = jnp.concatenate([mv_ref[pl.ds(start, L), :], ones_col], axis=1)
        brow = gsc[2 * direction, pl.ds(c, 1), :]
        rrow = gsc[2 * direction + 1, pl.ds(c, 1), :]
        bcol = jnp.sum(jnp.where(row_i == col_i, brow, 0.0), axis=1, keepdims=True)
        dmat = jnp.where(causal, bcol + rrow, -jnp.inf)
        dmax = jnp.max(dmat, axis=1, keepdims=True)
        c_old = c_s[direction]
        q_out = jnp.dot(q, jnp.concatenate([kt, c_old.astype(BF16)], axis=1),
                        preferred_element_type=F32)
        qk, p_inter = q_out[:, :L], q_out[:, L:]
        st = (qk * jnp.exp(dmat - dmax)).astype(BF16)
        b_last = jnp.sum(jnp.where(lane_row == last_lane, brow, 0.0), axis=1, keepdims=True)
        grow = b_last + rrow
        gmax = jnp.max(grow, axis=1, keepdims=True)
        wk = jnp.exp(grow - gmax)
        ktw = (kt.astype(F32) * wk).astype(BF16)
        v_out = jnp.dot(jnp.concatenate([st, ktw], axis=0), vaug, preferred_element_type=F32)
        p_intra, upd = v_out[:L], v_out[L:]
        inter = bcol + m
        m_t = jnp.maximum(inter, dmax)
        num_aug = jnp.exp(inter - m_t) * p_inter + jnp.exp(dmax - m_t) * p_intra
        den = num_aug[:, dh:dh + 1]
        hout = num_aug[:, :dh] / jnp.maximum(jnp.abs(den), jnp.exp(-m_t))
        m_new = jnp.maximum(b_last + m, gmax)
        c_s[direction] = jnp.exp(b_last + m - m_new) * c_old + jnp.exp(gmax - m_new) * upd
        return hout, m_new

    def finalize(c, hm):
        start = pl.multiple_of(c * L, L)
        y = _rms(hm, hg_ref[...])
        og = jax.nn.sigmoid(mo_ref[pl.ds(start, L), :].astype(F32))
        o_ref[pl.ds(start, L), :] = (og * y).astype(BF16)

    def run_half(first_step, ms, second_half):
        def body(k, ms):
            out = []
            for direction in range(2):
                hout, m_new = chunk_step(k, ms[direction], direction)
                rows = pl.ds(pl.multiple_of(chunk_of(k, direction) * L, L), L)
                if second_half:
                    finalize(chunk_of(k, direction), hout + hbuf[rows, :])
                else:
                    hbuf[rows, :] = hout
                out.append(m_new)
            return tuple(out)

        return lax.fori_loop(first_step, first_step + nc // 2, body, ms, unroll=2)

    m0 = jnp.zeros((1, 1), F32)
    ms = run_half(0, (m0, m0), False)
    run_half(nc // 2, ms, True)


def _mlstm(proj, gates, conv_w, conv_b, head_g):
    nb, seq, _ = proj.shape
    dh = MLSTM_HEAD_DIM
    nc = seq // MLSTM_CHUNK
    assert seq % MLSTM_CHUNK == 0 and nc % 2 == 0
    base = 3 * ATT_WIDTH // dh
    nh = MLSTM_HEADS

    def col(group):
        return pl.BlockSpec((None, seq, dh), lambda b, h: (b, 0, base + group * nh + h))

    return pl.pallas_call(
        _mlstm_body,
        out_shape=jax.ShapeDtypeStruct((nb, seq, MLSTM_WIDTH), BF16),
        grid=(nb, nh),
        in_specs=[
            col(0), col(1), col(2), col(3),
            pl.BlockSpec((CONV_WIDTH, dh), lambda b, h: (0, h)),
            pl.BlockSpec((CONV_WIDTH, dh), lambda b, h: (0, nh + h)),
            pl.BlockSpec((1, dh), lambda b, h: (0, h)),
            pl.BlockSpec((1, dh), lambda b, h: (0, nh + h)),
            pl.BlockSpec((4, None, None, nc, MLSTM_CHUNK), lambda b, h: (0, h, b, 0, 0)),
            pl.BlockSpec((1, dh), lambda b, h: (0, h)),
        ],
        out_specs=pl.BlockSpec((None, seq, dh), lambda b, h: (b, 0, h)),
        scratch_shapes=[
            pltpu.VMEM((seq, dh), BF16),
            pltpu.VMEM((nc, dh, MLSTM_CHUNK), BF16),
            pltpu.VMEM((4, nc, MLSTM_CHUNK), F32),
            pltpu.VMEM((seq, dh), F32),
            pltpu.VMEM((2, dh, dh + LANES), F32),
        ],
        compiler_params=_params(2),
        name="mlstm",
    )(proj, proj, proj, proj, conv_w, conv_w, conv_b, conv_b, gates, head_g)


def _outproj_body(att_ref, ml_ref, wa_ref, wm_ref, x_ref, g_ref, o_ref):
    mixed = (jnp.dot(att_ref[...], wa_ref[...], preferred_element_type=F32)
             + jnp.dot(ml_ref[...], wm_ref[...], preferred_element_type=F32))
    o_ref[...] = x_ref[...] + _rms(mixed, g_ref[...])


def _out_proj(att, ml, w_out, x, g, *, tm):
    t, d = x.shape
    wa = att.shape[1]
    wm = ml.shape[1]
    assert wa == wm
    return pl.pallas_call(
        _outproj_body,
        out_shape=jax.ShapeDtypeStruct((t, d), F32),
        grid=(t // tm,),
        in_specs=[
            pl.BlockSpec((tm, wa), lambda i: (i, 0)),
            pl.BlockSpec((tm, wm), lambda i: (i, 0)),
            pl.BlockSpec((wa, d), lambda i: (0, 0)),
            pl.BlockSpec((wm, d), lambda i: (1, 0)),
            pl.BlockSpec((tm, d), lambda i: (i, 0)),
            pl.BlockSpec((1, d), lambda i: (0, 0)),
        ],
        out_specs=pl.BlockSpec((tm, d), lambda i: (i, 0)),
        compiler_params=_params(1),
        name="out_proj",
    )(att, ml, w_out, w_out, x, g)


def _layer(xs, rel_bias, ffn1_pre_g, ffn1_post_g, ffn1_w_gate, ffn1_w_up, ffn1_w_down,
           mix_pre_g, mix_post_g, w_in, gate_bias, conv_w, conv_b, lam_vecs,
           att_head_g, mlstm_head_g, w_out,
           ffn2_pre_g, ffn2_post_g, ffn2_w_gate, ffn2_w_up, ffn2_w_down,
           *, tm, tm_proj, tf, tn, attn_tile):
    seq, d = xs[0].shape[1:]
    rows = tuple(x.shape[0] * seq for x in xs)
    t = sum(rows)
    nb = t // seq
    row = lambda v: v.reshape(1, -1).astype(F32)

    (x1,) = _ffn(tuple(x.reshape(-1, d) for x in xs), row(ffn1_pre_g), row(ffn1_post_g),
                 ffn1_w_gate.astype(BF16), ffn1_w_up.astype(BF16), ffn1_w_down.astype(BF16), (t,),
                 tm=tm, tf=tf)

    w_main = w_in.astype(BF16)
    w_gate = jnp.pad(w_main[:, MAIN_COLS:], ((0, 0), (0, LANES - N_GATES)))
    gbias = jnp.pad(row(gate_bias), ((0, 0), (0, LANES - N_GATES)))
    col_scale = jnp.concatenate([jnp.full((1, ATT_WIDTH), ATT_QK_DIM ** -0.5 * LOG2E, F32),
                                 jnp.ones((1, MAIN_COLS - ATT_WIDTH), F32)], axis=1)
    proj, gates = _in_proj(x1, row(mix_pre_g), w_main, col_scale, w_gate, gbias, tm=tm_proj, tn=tn)
    proj = proj.reshape(nb, seq, MAIN_COLS)
    nc = seq // MLSTM_CHUNK
    gates = gates[:N_GATES].reshape(N_GATES // MLSTM_HEADS, MLSTM_HEADS, nb, nc, MLSTM_CHUNK)

    btiles = _bias_tiles(rel_bias.astype(F32))
    att = _attention(proj, btiles, rel_bias.astype(F32), lam_vecs, row(att_head_g), tile=attn_tile)
    ml = _mlstm(proj, gates, conv_w.astype(F32), row(conv_b), row(mlstm_head_g))

    x2 = _out_proj(att.reshape(t, ATT_WIDTH), ml.reshape(t, MLSTM_WIDTH), w_out.astype(BF16),
                   x1, row(mix_post_g), tm=tm)
    ys = _ffn((x2,), row(ffn2_pre_g), row(ffn2_post_g), ffn2_w_gate.astype(BF16),
              ffn2_w_up.astype(BF16), ffn2_w_down.astype(BF16), rows, tm=tm, tf=tf)
    return tuple(y.reshape(x.shape) for y, x in zip(ys, xs))


def kernel(x_prompt, x_sample, rel_bias, ffn1_pre_g, ffn1_post_g, ffn1_w_gate, ffn1_w_up, ffn1_w_down, mix_pre_g, mix_post_g, w_in, gate_bias, conv_w, conv_b, lambda_q1, lambda_k1, lambda_q2, lambda_k2, att_head_g, mlstm_head_g, w_out, ffn2_pre_g, ffn2_post_g, ffn2_w_gate, ffn2_w_up, ffn2_w_down, *, tm=512, tm_proj=1024, tf=512, tn=1792, attn_tile=512):
    assert x_prompt.shape[1:] == x_sample.shape[1:]
    assert ffn1_pre_g.shape[0] == 1, "single-layer trunk"
    lam_vecs = jnp.concatenate([lambda_q1, lambda_k1, lambda_q2, lambda_k2], axis=0).astype(F32)
    return _layer((x_prompt, x_sample), rel_bias, ffn1_pre_g[0], ffn1_post_g[0], ffn1_w_gate[0], ffn1_w_up[0], ffn1_w_down[0],
               mix_pre_g[0], mix_post_g[0], w_in[0], gate_bias[0], conv_w[0], conv_b[0], lam_vecs,
               att_head_g[0], mlstm_head_g[0], w_out[0],
               ffn2_pre_g[0], ffn2_post_g[0], ffn2_w_gate[0], ffn2_w_up[0], ffn2_w_down[0],
               tm=tm, tm_proj=tm_proj, tf=tf, tn=tn, attn_tile=attn_tile)
```

```python
import functools
import math

import jax
import jax.numpy as jnp
from jax import lax
from jax.experimental import pallas as pl
from jax.experimental.pallas import tpu as pltpu

F32 = jnp.float32
BF16 = jnp.bfloat16

EPS = 1e-6
ATT_HEADS = 8
ATT_QK_DIM = 64
ATT_V_DIM = 128
ATT_WIDTH = ATT_HEADS * ATT_V_DIM
MLSTM_HEADS = 4
MLSTM_HEAD_DIM = 256
MLSTM_WIDTH = MLSTM_HEADS * MLSTM_HEAD_DIM
MLSTM_CHUNK = 128
CONV_WIDTH = 5
N_GATES = 4 * MLSTM_HEADS
MAIN_COLS = 3 * ATT_WIDTH + 4 * MLSTM_WIDTH
REL_BUCKETS = 32
REL_MAX_DIST = 128
LAMBDA_INIT = 0.8 - 0.6 * math.exp(-0.3 * 0)
LOG2E = math.log2(math.e)

LANES = 128
BF16_ROWS = 16
VMEM_LIMIT = 60 * 1024 * 1024
assert REL_MAX_DIST <= LANES


def _params(n_axes):
    return pltpu.CompilerParams(dimension_semantics=("arbitrary",) * n_axes,
                                vmem_limit_bytes=VMEM_LIMIT)


def _rms(xf, g_row):
    ms = jnp.mean(xf * xf, axis=-1, keepdims=True)
    return xf * lax.rsqrt(ms + EPS) * g_row


def _segment_of(i, tile_starts):
    seg = 0
    for start in tile_starts[1:]:
        seg = seg + (i >= start).astype(jnp.int32)
    return seg


def _ffn_body(*refs, in_starts, out_starts, tf, tiles_per_cast, cast_stage):
    n_in, n_out, n_casts = len(in_starts), len(out_starts), len(cast_stage)
    n_stage = len(set(cast_stage))
    x_refs = refs[:n_in]
    pre_g_ref, post_g_ref, wg_hbm, wu_hbm, wd_hbm = refs[n_in:n_in + 5]
    cast_src = refs[n_in + 5:n_in + 5 + n_casts]
    o_refs = refs[n_in + 5 + n_casts:n_in + 5 + n_casts + n_out]
    cast_dst = refs[n_in + 5 + n_casts + n_out:n_in + 5 + 2 * n_casts + n_out]
    scratch = refs[n_in + 5 + 2 * n_casts + n_out:]
    h_ref, acc_ref, wg_buf, wu_buf, wd_buf, sem = scratch[:6]
    stage32, stage16 = scratch[6:6 + n_stage], scratch[6 + n_stage:6 + 2 * n_stage]
    n_chunks = wg_hbm.shape[1] // tf
    i = pl.program_id(0)
    n_tiles = pl.num_programs(0)
    in_seg = _segment_of(i, in_starts)
    out_seg = _segment_of(i, out_starts)

    def cast_step(step):
        for j in range(n_casts):
            @pl.when(i // tiles_per_cast == j)
            def _(j=j):
                cast_sem = scratch[6 + 2 * n_stage]
                rows = cast_src[j].shape[0] // tiles_per_cast
                slab = pl.ds(pl.multiple_of((i - j * tiles_per_cast) * rows, rows), rows)
                s32, s16 = stage32[cast_stage[j]], stage16[cast_stage[j]]
                slab_in = pltpu.make_async_copy(cast_src[j].at[slab, :], s32, cast_sem.at[0])
                slab_out = pltpu.make_async_copy(s16, cast_dst[j].at[slab, :], cast_sem.at[1])
                if step == 0:
                    slab_in.start()
                elif step == 1:
                    slab_in.wait()
                    s16[...] = s32[...].astype(BF16)
                    slab_out.start()
                else:
                    slab_out.wait()

    def chunk_copies(f, slot):
        cols = pl.ds(f * tf, tf)
        return (pltpu.make_async_copy(wg_hbm.at[:, cols], wg_buf.at[slot], sem.at[0, slot]),
                pltpu.make_async_copy(wu_hbm.at[:, cols], wu_buf.at[slot], sem.at[1, slot]),
                pltpu.make_async_copy(wd_hbm.at[cols, :], wd_buf.at[slot], sem.at[2, slot]))

    first_slot = (i * n_chunks) % 2

    @pl.when(i == 0)
    def _():
        for cp in chunk_copies(0, 0):
            cp.start()

    cast_step(0)

    for k, x_ref in enumerate(x_refs):
        @pl.when(in_seg == k)
        def _(x_ref=x_ref):
            h_ref[...] = _rms(x_ref[...], pre_g_ref[...]).astype(BF16)

    h = h_ref[...]
    for f in range(n_chunks):
        slot = (first_slot + f) % 2
        for cp in chunk_copies((f + 1) % n_chunks, 1 - slot):
            cp.start()
        for cp in chunk_copies(f, slot):
            cp.wait()
        g = jnp.dot(h, wg_buf[slot], preferred_element_type=F32)
        u = jnp.dot(h, wu_buf[slot], preferred_element_type=F32)
        a = (g * jax.nn.sigmoid(g) * u).astype(BF16)
        part = jnp.dot(a, wd_buf[slot], preferred_element_type=F32)
        if f == 0:
            acc_ref[...] = part
        else:
            acc_ref[...] += part

    cast_step(1)

    for k, x_ref in enumerate(x_refs):
        for m, o_ref in enumerate(o_refs):
            @pl.when(jnp.logical_and(in_seg == k, out_seg == m))
            def _(x_ref=x_ref, o_ref=o_ref):
                o_ref[...] = x_ref[...] + _rms(acc_ref[...], 0.5 * post_g_ref[...])

    cast_step(2)

    @pl.when(i == n_tiles - 1)
    def _():
        for cp in chunk_copies(0, (first_slot + n_chunks) % 2):
            cp.wait()


def _ffn(xs, pre_g, post_g, wg, wu, wd, out_rows, *, tm, tf, casts=()):
    d = xs[0].shape[1]
    fdim = wg.shape[1]
    assert fdim % tf == 0
    n_tiles = sum(x.shape[0] for x in xs) // tm
    tiles_per_cast = n_tiles // len(casts) if casts else n_tiles
    slab_shapes = []
    for w in casts:
        assert n_tiles % len(casts) == 0 and w.shape[0] % (tiles_per_cast * BF16_ROWS) == 0
        slab_shapes.append((w.shape[0] // tiles_per_cast, w.shape[1]))
    stage_shapes = sorted(set(slab_shapes))
    cast_stage = tuple(stage_shapes.index(sh) for sh in slab_shapes)

    def tile_starts(rows):
        assert all(r % tm == 0 for r in rows)
        starts = [0]
        for r in rows[:-1]:
            starts.append(starts[-1] + r // tm)
        return tuple(starts)

    in_rows = tuple(x.shape[0] for x in xs)
    assert sum(in_rows) == sum(out_rows)
    in_starts, out_starts = tile_starts(in_rows), tile_starts(out_rows)

    def seg_spec(start, rows):
        n = rows // tm
        return pl.BlockSpec((tm, d), lambda i: (jnp.clip(i - start, 0, n - 1), 0))

    fixed = lambda i: (0, 0)
    hbm = pl.BlockSpec(memory_space=pl.ANY)
    return pl.pallas_call(
        functools.partial(_ffn_body, in_starts=in_starts, out_starts=out_starts, tf=tf,
                          tiles_per_cast=tiles_per_cast, cast_stage=cast_stage),
        out_shape=(tuple(jax.ShapeDtypeStruct((r, d), F32) for r in out_rows)
                   + tuple(jax.ShapeDtypeStruct(w.shape, BF16) for w in casts)),
        grid=(n_tiles,),
        in_specs=[seg_spec(s, r) for s, r in zip(in_starts, in_rows)] + [
            pl.BlockSpec((1, d), fixed),
            pl.BlockSpec((1, d), fixed),
            hbm, hbm, hbm,
        ] + [hbm] * len(casts),
        out_specs=tuple(seg_spec(s, r) for s, r in zip(out_starts, out_rows)) + (hbm,) * len(casts),
        scratch_shapes=[
            pltpu.VMEM((tm, d), BF16),
            pltpu.VMEM((tm, d), F32),
            pltpu.VMEM((2, d, tf), BF16),
            pltpu.VMEM((2, d, tf), BF16),
            pltpu.VMEM((2, tf, d), BF16),
            pltpu.SemaphoreType.DMA((3, 2)),
        ] + [pltpu.VMEM(sh, F32) for sh in stage_shapes] + [pltpu.VMEM(sh, BF16) for sh in stage_shapes]
        + ([pltpu.SemaphoreType.DMA((2,))] if casts else []),
        compiler_params=_params(1),
        name="ffn",
    )(*xs, pre_g, post_g, wg, wu, wd, *casts)


def _inproj_body(x_ref, g_ref, w_ref, cs_ref, wgate_ref, gbias_ref, o_ref, gates_ref, h_ref):
    j = pl.program_id(1)

    @pl.when(j == 0)
    def _():
        h = _rms(x_ref[...], g_ref[...]).astype(BF16)
        h_ref[...] = h
        gates = jnp.dot(h, wgate_ref[...], preferred_element_type=F32) + gbias_ref[...]
        gates_ref[...] = gates.T

    o_ref[...] = (jnp.dot(h_ref[...], w_ref[...], preferred_element_type=F32) * cs_ref[...]).astype(BF16)


def _in_proj(x, g, w_main, col_scale, w_gate, gate_bias, *, tm, tn):
    t, d = x.shape
    n = col_scale.shape[1]
    assert n % tn == 0 and n <= w_main.shape[1]
    return pl.pallas_call(
        _inproj_body,
        out_shape=(jax.ShapeDtypeStruct((t, n), BF16), jax.ShapeDtypeStruct((LANES, t), F32)),
        grid=(t // tm, n // tn),
        in_specs=[
            pl.BlockSpec((tm, d), lambda i, j: (i, 0)),
            pl.BlockSpec((1, d), lambda i, j: (0, 0)),
            pl.BlockSpec((d, tn), lambda i, j: (0, j)),
            pl.BlockSpec((1, tn), lambda i, j: (0, j)),
            pl.BlockSpec((d, LANES), lambda i, j: (0, 0)),
            pl.BlockSpec((1, LANES), lambda i, j: (0, 0)),
        ],
        out_specs=(pl.BlockSpec((tm, tn), lambda i, j: (i, j)),
                   pl.BlockSpec((LANES, tm), lambda i, j: (0, i))),
        scratch_shapes=[pltpu.VMEM((tm, d), BF16)],
        compiler_params=_params(2),
        name="in_proj",
    )(x, g, w_main, col_scale, w_gate, gate_bias)


def _rel_bucket(rel):
    nb = REL_BUCKETS // 2
    max_exact = nb // 2
    ret = jnp.where(rel > 0, nb, 0)
    n = jnp.abs(rel)
    nf = jnp.maximum(n, 1).astype(jnp.float32)
    large = max_exact + (jnp.log(nf / max_exact) / math.log(REL_MAX_DIST / max_exact)
                         * (nb - max_exact)).astype(jnp.int32)
    large = jnp.minimum(large, nb - 1)
    return ret + jnp.where(n < max_exact, n, large)


def _bias_tiles_body(relb_ref, bucket_ref, o_ref):
    hc = pl.program_id(0)
    bucket = bucket_ref[...]
    acc = jnp.zeros(bucket.shape, F32)
    for b in range(REL_BUCKETS):
        acc = jnp.where(bucket == b, relb_ref[b, hc], acc)
    o_ref[...] = acc * LOG2E


def _bias_tiles(rel_bias):
    key = lax.broadcasted_iota(jnp.int32, (3, LANES, LANES), 1)
    query = lax.broadcasted_iota(jnp.int32, (3, LANES, LANES), 2)
    off = (lax.broadcasted_iota(jnp.int32, (3, LANES, LANES), 0) - 1) * LANES
    bucket = _rel_bucket(off + key - query).astype(jnp.int32)
    n_hc = rel_bias.shape[1]
    return pl.pallas_call(
        _bias_tiles_body,
        out_shape=jax.ShapeDtypeStruct((n_hc, 3, LANES, LANES), F32),
        grid=(n_hc,),
        in_specs=[
            pl.BlockSpec(memory_space=pltpu.SMEM),
            pl.BlockSpec((3, LANES, LANES), lambda i: (0, 0, 0)),
        ],
        out_specs=pl.BlockSpec((None, 3, LANES, LANES), lambda i: (i, 0, 0, 0)),
        compiler_params=_params(1),
        name="rel_bias_tiles",
    )(rel_bias, bucket)


V_ROWS = ATT_V_DIM + BF16_ROWS


def _attn_body(relb_ref, lam_ref, q_ref, k_ref, v_ref, btile_ref, hg_ref, o_ref,
               qz_ref, vt_ref, bias_ref, sa_ref, sb_ref, ma_ref, mb_ref, ha_ref, hb_ref, acc_ref, m_ref, *, tile):
    h = pl.program_id(0)
    seq = k_ref.shape[0]
    n_tiles = seq // tile
    n_sub = tile // LANES
    far_bucket_neg = REL_BUCKETS // 2 - 1
    far_bucket_pos = REL_BUCKETS - 1

    q = q_ref[...]
    lane = lax.broadcasted_iota(jnp.int32, q.shape, 1)
    zero = jnp.zeros_like(q)
    qz_ref[0] = jnp.where(lane < ATT_QK_DIM, q, zero)
    qz_ref[1] = jnp.where(lane >= ATT_QK_DIM, q, zero)
    ones_rows = jnp.where(lax.broadcasted_iota(jnp.int32, (V_ROWS - ATT_V_DIM, tile), 0) == 0,
                          1.0, 0.0).astype(BF16)
    for jt in range(n_tiles):
        vt_ref[jt, :ATT_V_DIM, :] = v_ref[jt * tile:(jt + 1) * tile, :].T
        vt_ref[jt, ATT_V_DIM:, :] = ones_rows
    @pl.when(pl.program_id(1) == 0)
    def _():
        for c in range(2):
            cneg = relb_ref[far_bucket_neg, 2 * h + c] * LOG2E
            cpos = relb_ref[far_bucket_pos, 2 * h + c] * LOG2E
            for di in range(3):
                for kj in range(n_sub):
                    for qi in range(n_sub):
                        d = (di - 1) * n_sub + kj - qi
                        if -1 <= d <= 1:
                            sub = btile_ref[c, d + 1]
                        else:
                            sub = jnp.full((LANES, LANES), cneg if d < 0 else cpos, F32)
                        bias_ref[c, di, kj * LANES:(kj + 1) * LANES, qi * LANES:(qi + 1) * LANES] = sub

    m_ref[...] = jnp.full(m_ref.shape, -jnp.inf, F32)
    acc_ref[...] = jnp.zeros(acc_ref.shape, F32)

    lam_v = lam_ref[...]
    s1 = jnp.sum(lam_v[0:1] * lam_v[1:2], axis=-1, keepdims=True)
    s2 = jnp.sum(lam_v[2:3] * lam_v[3:4], axis=-1, keepdims=True)
    lam = jnp.exp(s1) - jnp.exp(s2) + LAMBDA_INIT

    def scores(qi, j, bufs, near):
        s_ref, smax_ref, shift_ref = bufs
        kt = k_ref[pl.ds(pl.multiple_of(j * tile, tile), tile), :]
        for c in range(2):
            qz = qz_ref[c, pl.ds(pl.multiple_of(qi * tile, tile), tile), :]
            s = lax.dot_general(kt, qz, (((1,), (1,)), ((), ())), preferred_element_type=F32)
            if near:
                s = s + bias_ref[c, j - qi + 1]
                shift = jnp.zeros((1, tile), F32)
            else:
                shift = jnp.full((1, tile), jnp.where(j < qi, relb_ref[far_bucket_neg, 2 * h + c],
                                                      relb_ref[far_bucket_pos, 2 * h + c]) * LOG2E, F32)
            s_ref[c] = s
            smax_ref[c] = jnp.max(s, axis=0, keepdims=True) + shift
            shift_ref[c] = shift

    def accumulate(j, bufs):
        s_ref, smax_ref, shift_ref = bufs
        vt = vt_ref[j]
        for c in range(2):
            m_old = m_ref[c]
            m_new = jnp.maximum(m_old, smax_ref[c])
            alpha = jnp.exp2(m_old - m_new)
            e = jnp.exp2(s_ref[c] - (m_new - shift_ref[c])).astype(BF16)
            acc_ref[c] = alpha * acc_ref[c] + jnp.dot(vt, e, preferred_element_type=F32)
            m_ref[c] = m_new

    def finalize(qi):
        outs = []
        for c in range(2):
            acc = acc_ref[c]
            outs.append(acc[:ATT_V_DIM] / acc[ATT_V_DIM:ATT_V_DIM + 1])
        o_t = outs[0] - lam * outs[1]
        y_t = o_t * lax.rsqrt(jnp.mean(o_t * o_t, axis=0, keepdims=True) + EPS)
        o_ref[pl.ds(pl.multiple_of(qi * tile, tile), tile), :] = (
            y_t.T * hg_ref[...] * (1.0 - LAMBDA_INIT)).astype(BF16)
        m_ref[...] = jnp.full(m_ref.shape, -jnp.inf, F32)
        acc_ref[...] = jnp.zeros(acc_ref.shape, F32)

    buf_a = (sa_ref, ma_ref, ha_ref)
    buf_b = (sb_ref, mb_ref, hb_ref)

    def is_near(qi, j):
        return jnp.abs(j - qi) <= 1

    def pair(q1, j1, q2, j2, jc1, jc2, tail=None):
        n1, n2 = is_near(q1, j1), is_near(q2, j2)
        for v1 in (True, False):
            for v2 in (True, False):
                @pl.when(jnp.logical_and(n1 == v1, n2 == v2))
                def _(v1=v1, v2=v2):
                    scores(q1, j1, buf_b, v1)
                    accumulate(jc1, buf_a)
                    scores(q2, j2, buf_a, v2)
                    accumulate(jc2, buf_b)
                    if tail is not None:
                        tail()

    scores(0, 0, buf_a, True)

    def q_body(qi, carry):
        def pair_body(t, c2):
            j = 2 * t
            pair(qi, j + 1, qi, j + 2, j, j + 1)
            return c2

        lax.fori_loop(0, n_tiles // 2 - 1, pair_body, 0)
        pair(qi, n_tiles - 1, jnp.minimum(qi + 1, n_tiles - 1), 0, n_tiles - 2, n_tiles - 1,
             tail=lambda: finalize(qi))
        return carry

    lax.fori_loop(0, n_tiles, q_body, 0)


def _attention(proj, btiles, rel_bias, lam_vecs, head_g, *, tile):
    nb, seq, _ = proj.shape
    assert seq % (2 * tile) == 0 and tile % LANES == 0
    btiles = btiles.reshape(ATT_HEADS, 2, 3, LANES, LANES)
    return pl.pallas_call(
        functools.partial(_attn_body, tile=tile),
        out_shape=jax.ShapeDtypeStruct((nb, seq, ATT_WIDTH), BF16),
        grid=(ATT_HEADS, nb),
        in_specs=[
            pl.BlockSpec(memory_space=pltpu.SMEM),
            pl.BlockSpec((4, ATT_QK_DIM), lambda h, b: (0, 0)),
            pl.BlockSpec((None, seq, LANES), lambda h, b: (b, 0, h)),
            pl.BlockSpec((None, seq, LANES), lambda h, b: (b, 0, ATT_HEADS + h)),
            pl.BlockSpec((None, seq, LANES), lambda h, b: (b, 0, 2 * ATT_HEADS + h)),
            pl.BlockSpec((None, 2, 3, LANES, LANES), lambda h, b: (h, 0, 0, 0, 0)),
            pl.BlockSpec((1, ATT_V_DIM), lambda h, b: (0, h)),
        ],
        out_specs=pl.BlockSpec((None, seq, ATT_V_DIM), lambda h, b: (b, 0, h)),
        scratch_shapes=[
            pltpu.VMEM((2, seq, LANES), BF16),
            pltpu.VMEM((seq // tile, V_ROWS, tile), BF16),
            pltpu.VMEM((2, 3, tile, tile), F32),
            pltpu.VMEM((2, tile, tile), F32),
            pltpu.VMEM((2, tile, tile), F32),
            pltpu.VMEM((2, 1, tile), F32),
            pltpu.VMEM((2, 1, tile), F32),
            pltpu.VMEM((2, 1, tile), F32),
            pltpu.VMEM((2, 1, tile), F32),
            pltpu.VMEM((2, V_ROWS, tile), F32),
            pltpu.VMEM((2, 1, tile), F32),
        ],
        compiler_params=_params(2),
        name="diff_attention",
    )(rel_bias, lam_vecs, proj, proj, proj, btiles, head_g)


def _log_sigmoid(x):
    return jnp.minimum(x, 0.0) - jnp.log1p(jnp.exp(-jnp.abs(x)))


def _lane_scan(x, reverse):
    lane = lax.broadcasted_iota(jnp.int32, x.shape, 1)
    sh = 1
    while sh < LANES:
        if reverse:
            x = x + jnp.where(lane < LANES - sh, pltpu.roll(x, LANES - sh, 1), 0.0)
        else:
            x = x + jnp.where(lane >= sh, pltpu.roll(x, sh, 1), 0.0)
        sh *= 2
    return x


def _mlstm_body(mq_ref, mk_ref, mv_ref, mo_ref, cwq_ref, cwk_ref, cbq_ref, cbk_ref, gates_ref, hg_ref,
                o_ref, q_s, kt_s, gsc, hbuf, c_s):
    seq, dh = mq_ref.shape
    L = MLSTM_CHUNK
    nc = seq // L
    halo = BF16_ROWS

    def conv_chunk(src_ref, w_ref, b_ref, c):
        start = pl.multiple_of(c * L, L)
        cur = src_ref[pl.ds(start, L), :].astype(F32)
        prev_start = pl.multiple_of(jnp.maximum(start - halo, 0), halo)
        next_start = pl.multiple_of(jnp.minimum(start + L, seq - halo), halo)
        prev = src_ref[pl.ds(prev_start, halo), :].astype(F32) * jnp.where(c > 0, 1.0, 0.0)
        nxt = src_ref[pl.ds(next_start, halo), :].astype(F32) * jnp.where(c < nc - 1, 1.0, 0.0)
        ext = jnp.concatenate([prev, cur, nxt], axis=0)
        w = w_ref[...]
        out = jnp.broadcast_to(b_ref[...], (L, dh))
        pad = CONV_WIDTH // 2
        for t in range(CONV_WIDTH):
            lo = halo + t - pad
            out = out + ext[lo:lo + L, :] * w[t:t + 1, :]
        return out * jax.nn.sigmoid(out)

    def prep(c, carry):
        start = pl.multiple_of(c * L, L)
        qc = conv_chunk(mq_ref, cwq_ref, cbq_ref, c) * (MLSTM_HEAD_DIM ** -0.5)
        q_s[pl.ds(start, L), :] = qc.astype(BF16)
        kc = conv_chunk(mk_ref, cwk_ref, cbk_ref, c)
        kt_s[c] = kc.T.astype(BF16)
        return carry

    lax.fori_loop(0, nc, prep, 0)

    g = gates_ref[...]
    b_f = _lane_scan(_log_sigmoid(g[2]), reverse=False)
    b_b = _lane_scan(_log_sigmoid(g[3]), reverse=True)
    gsc[0] = b_f
    gsc[1] = g[0] - b_f
    gsc[2] = b_b
    gsc[3] = g[1] - b_b

    c_s[...] = jnp.zeros(c_s.shape, F32)

    row_i = lax.broadcasted_iota(jnp.int32, (L, L), 0)
    col_i = lax.broadcasted_iota(jnp.int32, (L, L), 1)
    lane_row = lax.broadcasted_iota(jnp.int32, (1, L), 1)
    ones_col = jnp.where(lax.broadcasted_iota(jnp.int32, (L, LANES), 1) == 0, 1.0, 0.0).astype(BF16)

    def chunk_of(k, direction):
        return k if direction == 0 else nc - 1 - k

    def chunk_step(k, m, direction):
        c = chunk_of(k, direction)
        start = pl.multiple_of(c * L, L)
        causal = (col_i <= row_i) if direction == 0 else (col_i >= row_i)
        last_lane = L - 1 if direction == 0 else 0
        q = q_s[pl.ds(start, L), :]
        kt = kt_s[c]
        vaug = jnp.concatenate([mv_ref[pl.ds(start, L), :], ones_col], axis=1)
        brow = gsc[2 * direction, pl.ds(c, 1), :]
        rrow = gsc[2 * direction + 1, pl.ds(c, 1), :]
        bcol = jnp.sum(jnp.where(row_i == col_i, brow, 0.0), axis=1, keepdims=True)
        dmat = jnp.where(causal, bcol + rrow, -jnp.inf)
        dmax = jnp.max(dmat, axis=1, keepdims=True)
        c_old = c_s[direction]
        q_out = jnp.dot(q, jnp.concatenate([kt, c_old.astype(BF16)], axis=1),
                        preferred_element_type=F32)
        qk, p_inter = q_out[:, :L], q_out[:, L:]
        st = (qk * jnp.exp(dmat - dmax)).astype(BF16)
        b_last = jnp.sum(jnp.where(lane_row == last_lane, brow, 0.0), axis=1, keepdims=True)
        grow = b_last + rrow
        gmax = jnp.max(grow, axis=1, keepdims=True)
        wk = jnp.exp(grow - gmax)
        ktw = (kt.astype(F32) * wk).astype(BF16)
        v_out = jnp.dot(jnp.concatenate([st, ktw], axis=0), vaug, preferred_element_type=F32)
        p_intra, upd = v_out[:L], v_out[L:]
        inter = bcol + m
        m_t = jnp.maximum(inter, dmax)
        num_aug = jnp.exp(inter - m_t) * p_inter + jnp.exp(dmax - m_t) * p_intra
        den = num_aug[:, dh:dh + 1]
        hout = num_aug[:, :dh] / jnp.maximum(jnp.abs(den), jnp.exp(-m_t))
        m_new = jnp.maximum(b_last + m, gmax)
        c_s[direction] = jnp.exp(b_last + m - m_new) * c_old + jnp.exp(gmax - m_new) * upd
        return hout, m_new

    def finalize(c, hm):
        start = pl.multiple_of(c * L, L)
        y = _rms(hm, hg_ref[...])
        og = jax.nn.sigmoid(mo_ref[pl.ds(start, L), :].astype(F32))
        o_ref[pl.ds(start, L), :] = (og * y).astype(BF16)

    def run_half(first_step, ms, second_half):
        def body(k, ms):
            out = []
            for direction in range(2):
                hout, m_new = chunk_step(k, ms[direction], direction)
                rows = pl.ds(pl.multiple_of(chunk_of(k, direction) * L, L), L)
                if second_half:
                    finalize(chunk_of(k, direction), hout + hbuf[rows, :])
                else:
                    hbuf[rows, :] = hout
                out.append(m_new)
            return tuple(out)

        return lax.fori_loop(first_step, first_step + nc // 2, body, ms, unroll=2)

    m0 = jnp.zeros((1, 1), F32)
    ms = run_half(0, (m0, m0), False)
    run_half(nc // 2, ms, True)


def _mlstm(proj, gates, conv_w, conv_b, head_g):
    nb, seq, _ = proj.shape
    dh = MLSTM_HEAD_DIM
    nc = seq // MLSTM_CHUNK
    assert seq % MLSTM_CHUNK == 0 and nc % 2 == 0
    base = 3 * ATT_WIDTH // dh
    nh = MLSTM_HEADS

    def col(group):
        return pl.BlockSpec((None, seq, dh), lambda b, h: (b, 0, base + group * nh + h))

    return pl.pallas_call(
        _mlstm_body,
        out_shape=jax.ShapeDtypeStruct((nb, seq, MLSTM_WIDTH), BF16),
        grid=(nb, nh),
        in_specs=[
            col(0), col(1), col(2), col(3),
            pl.BlockSpec((CONV_WIDTH, dh), lambda b, h: (0, h)),
            pl.BlockSpec((CONV_WIDTH, dh), lambda b, h: (0, nh + h)),
            pl.BlockSpec((1, dh), lambda b, h: (0, h)),
            pl.BlockSpec((1, dh), lambda b, h: (0, nh + h)),
            pl.BlockSpec((4, None, None, nc, MLSTM_CHUNK), lambda b, h: (0, h, b, 0, 0)),
            pl.BlockSpec((1, dh), lambda b, h: (0, h)),
        ],
        out_specs=pl.BlockSpec((None, seq, dh), lambda b, h: (b, 0, h)),
        scratch_shapes=[
            pltpu.VMEM((seq, dh), BF16),
            pltpu.VMEM((nc, dh, MLSTM_CHUNK), BF16),
            pltpu.VMEM((4, nc, MLSTM_CHUNK), F32),
            pltpu.VMEM((seq, dh), F32),
            pltpu.VMEM((2, dh, dh + LANES), F32),
        ],
        compiler_params=_params(2),
        name="mlstm",
    )(proj, proj, proj, proj, conv_w, conv_w, conv_b, conv_b, gates, head_g)


def _outproj_body(att_ref, ml_ref, wa_ref, wm_ref, x_ref, g_ref, o_ref):
    mixed = (jnp.dot(att_ref[...], wa_ref[...], preferred_element_type=F32)
             + jnp.dot(ml_ref[...], wm_ref[...], preferred_element_type=F32))
    o_ref[...] = x_ref[...] + _rms(mixed, g_ref[...])


def _out_proj(att, ml, w_out, x, g, *, tm):
    t, d = x.shape
    wa = att.shape[1]
    wm = ml.shape[1]
    assert wa == wm
    return pl.pallas_call(
        _outproj_body,
        out_shape=jax.ShapeDtypeStruct((t, d), F32),
        grid=(t // tm,),
        in_specs=[
            pl.BlockSpec((tm, wa), lambda i: (i, 0)),
            pl.BlockSpec((tm, wm), lambda i: (i, 0)),
            pl.BlockSpec((wa, d), lambda i: (0, 0)),
            pl.BlockSpec((wm, d), lambda i: (1, 0)),
            pl.BlockSpec((tm, d), lambda i: (i, 0)),
            pl.BlockSpec((1, d), lambda i: (0, 0)),
        ],
        out_specs=pl.BlockSpec((tm, d), lambda i: (i, 0)),
        compiler_params=_params(1),
        name="out_proj",
    )(att, ml, w_out, w_out, x, g)


def _layer(xs, rel_bias, ffn1_pre_g, ffn1_post_g, ffn1_w_gate, ffn1_w_up, ffn1_w_down,
           mix_pre_g, mix_post_g, w_in, gate_bias, conv_w, conv_b, lam_vecs,
           att_head_g, mlstm_head_g, w_out,
           ffn2_pre_g, ffn2_post_g, ffn2_w_gate, ffn2_w_up, ffn2_w_down,
           *, tm, tm_proj, tf, tn, attn_tile):
    seq, d = xs[0].shape[1:]
    rows = tuple(x.shape[0] * seq for x in xs)
    t = sum(rows)
    nb = t // seq
    row = lambda v: v.reshape(1, -1).astype(F32)

    x1, w2_gate, w2_up, w2_down = _ffn(
        tuple(x.reshape(-1, d) for x in xs), row(ffn1_pre_g), row(ffn1_post_g),
        ffn1_w_gate.astype(BF16), ffn1_w_up.astype(BF16), ffn1_w_down.astype(BF16), (t,),
        tm=tm, tf=tf, casts=(ffn2_w_gate, ffn2_w_up, ffn2_w_down))

    w_main = w_in.astype(BF16)
    w_gate = jnp.pad(w_main[:, MAIN_COLS:], ((0, 0), (0, LANES - N_GATES)))
    gbias = jnp.pad(row(gate_bias), ((0, 0), (0, LANES - N_GATES)))
    col_scale = jnp.concatenate([jnp.full((1, ATT_WIDTH), ATT_QK_DIM ** -0.5 * LOG2E, F32),
                                 jnp.ones((1, MAIN_COLS - ATT_WIDTH), F32)], axis=1)
    proj, gates = _in_proj(x1, row(mix_pre_g), w_main, col_scale, w_gate, gbias, tm=tm_proj, tn=tn)
    proj = proj.reshape(nb, seq, MAIN_COLS)
    nc = seq // MLSTM_CHUNK
    gates = gates[:N_GATES].reshape(N_GATES // MLSTM_HEADS, MLSTM_HEADS, nb, nc, MLSTM_CHUNK)

    btiles = _bias_tiles(rel_bias.astype(F32))
    att = _attention(proj, btiles, rel_bias.astype(F32), lam_vecs, row(att_head_g), tile=attn_tile)
    ml = _mlstm(proj, gates, conv_w.astype(F32), row(conv_b), row(mlstm_head_g))

    x2 = _out_proj(att.reshape(t, ATT_WIDTH), ml.reshape(t, MLSTM_WIDTH), w_out.astype(BF16),
                   x1, row(mix_post_g), tm=tm)
    ys = _ffn((x2,), row(ffn2_pre_g), row(ffn2_post_g), w2_gate, w2_up, w2_down, rows, tm=tm, tf=tf)
    return tuple(y.reshape(x.shape) for y, x in zip(ys, xs))


def kernel(x_prompt, x_sample, rel_bias, ffn1_pre_g, ffn1_post_g, ffn1_w_gate, ffn1_w_up, ffn1_w_down, mix_pre_g, mix_post_g, w_in, gate_bias, conv_w, conv_b, lambda_q1, lambda_k1, lambda_q2, lambda_k2, att_head_g, mlstm_head_g, w_out, ffn2_pre_g, ffn2_post_g, ffn2_w_gate, ffn2_w_up, ffn2_w_down, *, tm=512, tm_proj=1024, tf=512, tn=1792, attn_tile=512):
    assert x_prompt.shape[1:] == x_sample.shape[1:]
    assert ffn1_pre_g.shape[0] == 1, "single-layer trunk"
    lam_vecs = jnp.concatenate([lambda_q1, lambda_k1, lambda_q2, lambda_k2], axis=0).astype(F32)
    return _layer((x_prompt, x_sample), rel_bias, ffn1_pre_g[0], ffn1_post_g[0], ffn1_w_gate[0], ffn1_w_up[0], ffn1_w_down[0],
               mix_pre_g[0], mix_post_g[0], w_in[0], gate_bias[0], conv_w[0], conv_b[0], lam_vecs,
               att_head_g[0], mlstm_head_g[0], w_out[0],
               ffn2_pre_g[0], ffn2_post_g[0], ffn2_w_gate[0], ffn2_w_up[0], ffn2_w_down[0],
               tm=tm, tm_proj=tm_proj, tf=tf, tn=tn, attn_tile=attn_tile)
```

```python
import functools
import math

import jax
import jax.numpy as jnp
from jax import lax
from jax.experimental import pallas as pl
from jax.experimental.pallas import tpu as pltpu

F32 = jnp.float32
BF16 = jnp.bfloat16

EPS = 1e-6
ATT_HEADS = 8
ATT_QK_DIM = 64
ATT_V_DIM = 128
ATT_WIDTH = ATT_HEADS * ATT_V_DIM
MLSTM_HEADS = 4
MLSTM_HEAD_DIM = 256
MLSTM_WIDTH = MLSTM_HEADS * MLSTM_HEAD_DIM
MLSTM_CHUNK = 128
CONV_WIDTH = 5
N_GATES = 4 * MLSTM_HEADS
MAIN_COLS = 3 * ATT_WIDTH + 4 * MLSTM_WIDTH
REL_BUCKETS = 32
REL_MAX_DIST = 128
LAMBDA_INIT = 0.8 - 0.6 * math.exp(-0.3 * 0)
LOG2E = math.log2(math.e)

LANES = 128
BF16_ROWS = 16
VMEM_LIMIT = 60 * 1024 * 1024
assert REL_MAX_DIST <= LANES


def _params(n_axes):
    return pltpu.CompilerParams(dimension_semantics=("arbitrary",) * n_axes,
                                vmem_limit_bytes=VMEM_LIMIT)


def _rms(xf, g_row):
    ms = jnp.mean(xf * xf, axis=-1, keepdims=True)
    return xf * lax.rsqrt(ms + EPS) * g_row


def _segment_of(i, tile_starts):
    seg = 0
    for start in tile_starts[1:]:
        seg = seg + (i >= start).astype(jnp.int32)
    return seg


def _ffn_body(*refs, in_starts, out_starts, tf, tiles_per_cast, cast_stage):
    n_in, n_out, n_casts = len(in_starts), len(out_starts), len(cast_stage)
    n_stage = len(set(cast_stage))
    x_refs = refs[:n_in]
    pre_g_ref, post_g_ref, wg_hbm, wu_hbm, wd_hbm = refs[n_in:n_in + 5]
    cast_src = refs[n_in + 5:n_in + 5 + n_casts]
    o_refs = refs[n_in + 5 + n_casts:n_in + 5 + n_casts + n_out]
    cast_dst = refs[n_in + 5 + n_casts + n_out:n_in + 5 + 2 * n_casts + n_out]
    scratch = refs[n_in + 5 + 2 * n_casts + n_out:]
    h_ref, acc_ref, wg_buf, wu_buf, wd_buf, sem = scratch[:6]
    stage32, stage16 = scratch[6:6 + n_stage], scratch[6 + n_stage:6 + 2 * n_stage]
    n_chunks = wg_hbm.shape[1] // tf
    i = pl.program_id(0)
    n_tiles = pl.num_programs(0)
    in_seg = _segment_of(i, in_starts)
    out_seg = _segment_of(i, out_starts)

    def slab_copies(j, tile):
        cast_sem = scratch[6 + 2 * n_stage]
        rows = cast_src[j].shape[0] // tiles_per_cast
        slab = pl.ds(pl.multiple_of((tile - j * tiles_per_cast) * rows, rows), rows)
        s32, s16 = stage32[cast_stage[j]], stage16[cast_stage[j]]
        return (pltpu.make_async_copy(cast_src[j].at[slab, :], s32, cast_sem.at[0]),
                pltpu.make_async_copy(s16, cast_dst[j].at[slab, :], cast_sem.at[1]))

    def for_job_of(tile, fn, cond=True):
        for j in range(n_casts):
            pl.when(jnp.logical_and(cond, tile // tiles_per_cast == j))(functools.partial(fn, j))

    def cast_step(step):
        def convert(j):
            slab_in, slab_out = slab_copies(j, i)
            slab_in.wait()
            stage16[cast_stage[j]][...] = stage32[cast_stage[j]][...].astype(BF16)
            slab_out.start()

        if step == 0:
            for_job_of(i, lambda j: slab_copies(j, i)[0].start())
        elif step == 1:
            for_job_of(i - 1, lambda j: slab_copies(j, i - 1)[1].wait(), cond=i > 0)
            for_job_of(i, convert)
        else:
            for_job_of(i, lambda j: slab_copies(j, i)[1].wait(), cond=i == n_tiles - 1)

    def chunk_copies(f, slot):
        cols = pl.ds(f * tf, tf)
        return (pltpu.make_async_copy(wg_hbm.at[:, cols], wg_buf.at[slot], sem.at[0, slot]),
                pltpu.make_async_copy(wu_hbm.at[:, cols], wu_buf.at[slot], sem.at[1, slot]),
                pltpu.make_async_copy(wd_hbm.at[cols, :], wd_buf.at[slot], sem.at[2, slot]))

    first_slot = (i * n_chunks) % 2

    @pl.when(i == 0)
    def _():
        for cp in chunk_copies(0, 0):
            cp.start()

    cast_step(0)

    for k, x_ref in enumerate(x_refs):
        @pl.when(in_seg == k)
        def _(x_ref=x_ref):
            h_ref[...] = _rms(x_ref[...], pre_g_ref[...]).astype(BF16)

    h = h_ref[...]
    for f in range(n_chunks):
        slot = (first_slot + f) % 2
        for cp in chunk_copies((f + 1) % n_chunks, 1 - slot):
            cp.start()
        for cp in chunk_copies(f, slot):
            cp.wait()
        g = jnp.dot(h, wg_buf[slot], preferred_element_type=F32)
        u = jnp.dot(h, wu_buf[slot], preferred_element_type=F32)
        a = (g * jax.nn.sigmoid(g) * u).astype(BF16)
        part = jnp.dot(a, wd_buf[slot], preferred_element_type=F32)
        if f == 0:
            acc_ref[...] = part
        else:
            acc_ref[...] += part

    cast_step(1)

    for k, x_ref in enumerate(x_refs):
        for m, o_ref in enumerate(o_refs):
            @pl.when(jnp.logical_and(in_seg == k, out_seg == m))
            def _(x_ref=x_ref, o_ref=o_ref):
                o_ref[...] = x_ref[...] + _rms(acc_ref[...], 0.5 * post_g_ref[...])

    cast_step(2)

    @pl.when(i == n_tiles - 1)
    def _():
        for cp in chunk_copies(0, (first_slot + n_chunks) % 2):
            cp.wait()


def _ffn(xs, pre_g, post_g, wg, wu, wd, out_rows, *, tm, tf, casts=()):
    d = xs[0].shape[1]
    fdim = wg.shape[1]
    assert fdim % tf == 0
    n_tiles = sum(x.shape[0] for x in xs) // tm
    tiles_per_cast = n_tiles // len(casts) if casts else n_tiles
    slab_shapes = []
    for w in casts:
        assert n_tiles % len(casts) == 0 and w.shape[0] % (tiles_per_cast * BF16_ROWS) == 0
        slab_shapes.append((w.shape[0] // tiles_per_cast, w.shape[1]))
    stage_shapes = sorted(set(slab_shapes))
    cast_stage = tuple(stage_shapes.index(sh) for sh in slab_shapes)

    def tile_starts(rows):
        assert all(r % tm == 0 for r in rows)
        starts = [0]
        for r in rows[:-1]:
            starts.append(starts[-1] + r // tm)
        return tuple(starts)

    in_rows = tuple(x.shape[0] for x in xs)
    assert sum(in_rows) == sum(out_rows)
    in_starts, out_starts = tile_starts(in_rows), tile_starts(out_rows)

    def seg_spec(start, rows):
        n = rows // tm
        return pl.BlockSpec((tm, d), lambda i: (jnp.clip(i - start, 0, n - 1), 0))

    fixed = lambda i: (0, 0)
    hbm = pl.BlockSpec(memory_space=pl.ANY)
    return pl.pallas_call(
        functools.partial(_ffn_body, in_starts=in_starts, out_starts=out_starts, tf=tf,
                          tiles_per_cast=tiles_per_cast, cast_stage=cast_stage),
        out_shape=(tuple(jax.ShapeDtypeStruct((r, d), F32) for r in out_rows)
                   + tuple(jax.ShapeDtypeStruct(w.shape, BF16) for w in casts)),
        grid=(n_tiles,),
        in_specs=[seg_spec(s, r) for s, r in zip(in_starts, in_rows)] + [
            pl.BlockSpec((1, d), fixed),
            pl.BlockSpec((1, d), fixed),
            hbm, hbm, hbm,
        ] + [hbm] * len(casts),
        out_specs=tuple(seg_spec(s, r) for s, r in zip(out_starts, out_rows)) + (hbm,) * len(casts),
        scratch_shapes=[
            pltpu.VMEM((tm, d), BF16),
            pltpu.VMEM((tm, d), F32),
            pltpu.VMEM((2, d, tf), BF16),
            pltpu.VMEM((2, d, tf), BF16),
            pltpu.VMEM((2, tf, d), BF16),
            pltpu.SemaphoreType.DMA((3, 2)),
        ] + [pltpu.VMEM(sh, F32) for sh in stage_shapes] + [pltpu.VMEM(sh, BF16) for sh in stage_shapes]
        + ([pltpu.SemaphoreType.DMA((2,))] if casts else []),
        compiler_params=_params(1),
        name="ffn",
    )(*xs, pre_g, post_g, wg, wu, wd, *casts)


def _inproj_body(x_ref, g_ref, w_ref, cs_ref, wgate_ref, gbias_ref, o_ref, gates_ref, h_ref):
    j = pl.program_id(1)

    @pl.when(j == 0)
    def _():
        h = _rms(x_ref[...], g_ref[...]).astype(BF16)
        h_ref[...] = h
        gates = jnp.dot(h, wgate_ref[...], preferred_element_type=F32) + gbias_ref[...]
        gates_ref[...] = gates.T

    o_ref[...] = (jnp.dot(h_ref[...], w_ref[...], preferred_element_type=F32) * cs_ref[...]).astype(BF16)


def _in_proj(x, g, w_main, col_scale, w_gate, gate_bias, *, tm, tn):
    t, d = x.shape
    n = col_scale.shape[1]
    assert n % tn == 0 and n <= w_main.shape[1]
    return pl.pallas_call(
        _inproj_body,
        out_shape=(jax.ShapeDtypeStruct((t, n), BF16), jax.ShapeDtypeStruct((LANES, t), F32)),
        grid=(t // tm, n // tn),
        in_specs=[
            pl.BlockSpec((tm, d), lambda i, j: (i, 0)),
            pl.BlockSpec((1, d), lambda i, j: (0, 0)),
            pl.BlockSpec((d, tn), lambda i, j: (0, j)),
            pl.BlockSpec((1, tn), lambda i, j: (0, j)),
            pl.BlockSpec((d, LANES), lambda i, j: (0, 0)),
            pl.BlockSpec((1, LANES), lambda i, j: (0, 0)),
        ],
        out_specs=(pl.BlockSpec((tm, tn), lambda i, j: (i, j)),
                   pl.BlockSpec((LANES, tm), lambda i, j: (0, i))),
        scratch_shapes=[pltpu.VMEM((tm, d), BF16)],
        compiler_params=_params(2),
        name="in_proj",
    )(x, g, w_main, col_scale, w_gate, gate_bias)


def _rel_bucket(rel):
    nb = REL_BUCKETS // 2
    max_exact = nb // 2
    ret = jnp.where(rel > 0, nb, 0)
    n = jnp.abs(rel)
    nf = jnp.maximum(n, 1).astype(jnp.float32)
    large = max_exact + (jnp.log(nf / max_exact) / math.log(REL_MAX_DIST / max_exact)
                         * (nb - max_exact)).astype(jnp.int32)
    large = jnp.minimum(large, nb - 1)
    return ret + jnp.where(n < max_exact, n, large)


def _bias_tiles_body(relb_ref, bucket_ref, o_ref):
    hc = pl.program_id(0)
    bucket = bucket_ref[...]
    acc = jnp.zeros(bucket.shape, F32)
    for b in range(REL_BUCKETS):
        acc = jnp.where(bucket == b, relb_ref[b, hc], acc)
    o_ref[...] = acc * LOG2E


def _bias_tiles(rel_bias):
    key = lax.broadcasted_iota(jnp.int32, (3, LANES, LANES), 1)
    query = lax.broadcasted_iota(jnp.int32, (3, LANES, LANES), 2)
    off = (lax.broadcasted_iota(jnp.int32, (3, LANES, LANES), 0) - 1) * LANES
    bucket = _rel_bucket(off + key - query).astype(jnp.int32)
    n_hc = rel_bias.shape[1]
    return pl.pallas_call(
        _bias_tiles_body,
        out_shape=jax.ShapeDtypeStruct((n_hc, 3, LANES, LANES), F32),
        grid=(n_hc,),
        in_specs=[
            pl.BlockSpec(memory_space=pltpu.SMEM),
            pl.BlockSpec((3, LANES, LANES), lambda i: (0, 0, 0)),
        ],
        out_specs=pl.BlockSpec((None, 3, LANES, LANES), lambda i: (i, 0, 0, 0)),
        compiler_params=_params(1),
        name="rel_bias_tiles",
    )(rel_bias, bucket)


V_ROWS = ATT_V_DIM + BF16_ROWS


def _attn_body(relb_ref, lam_ref, q_ref, k_ref, v_ref, btile_ref, hg_ref, o_ref,
               qz_ref, vt_ref, bias_ref, sa_ref, sb_ref, ma_ref, mb_ref, ha_ref, hb_ref, acc_ref, m_ref, *, tile):
    h = pl.program_id(0)
    seq = k_ref.shape[0]
    n_tiles = seq // tile
    n_sub = tile // LANES
    far_bucket_neg = REL_BUCKETS // 2 - 1
    far_bucket_pos = REL_BUCKETS - 1

    q = q_ref[...]
    lane = lax.broadcasted_iota(jnp.int32, q.shape, 1)
    zero = jnp.zeros_like(q)
    qz_ref[0] = jnp.where(lane < ATT_QK_DIM, q, zero)
    qz_ref[1] = jnp.where(lane >= ATT_QK_DIM, q, zero)
    ones_rows = jnp.where(lax.broadcasted_iota(jnp.int32, (V_ROWS - ATT_V_DIM, tile), 0) == 0,
                          1.0, 0.0).astype(BF16)
    for jt in range(n_tiles):
        vt_ref[jt, :ATT_V_DIM, :] = v_ref[jt * tile:(jt + 1) * tile, :].T
        vt_ref[jt, ATT_V_DIM:, :] = ones_rows
    @pl.when(pl.program_id(1) == 0)
    def _():
        for c in range(2):
            cneg = relb_ref[far_bucket_neg, 2 * h + c] * LOG2E
            cpos = relb_ref[far_bucket_pos, 2 * h + c] * LOG2E
            for di in range(3):
                for kj in range(n_sub):
                    for qi in range(n_sub):
                        d = (di - 1) * n_sub + kj - qi
                        if -1 <= d <= 1:
                            sub = btile_ref[c, d + 1]
                        else:
                            sub = jnp.full((LANES, LANES), cneg if d < 0 else cpos, F32)
                        bias_ref[c, di, kj * LANES:(kj + 1) * LANES, qi * LANES:(qi + 1) * LANES] = sub

    m_ref[...] = jnp.full(m_ref.shape, -jnp.inf, F32)
    acc_ref[...] = jnp.zeros(acc_ref.shape, F32)

    lam_v = lam_ref[...]
    s1 = jnp.sum(lam_v[0:1] * lam_v[1:2], axis=-1, keepdims=True)
    s2 = jnp.sum(lam_v[2:3] * lam_v[3:4], axis=-1, keepdims=True)
    lam = jnp.exp(s1) - jnp.exp(s2) + LAMBDA_INIT

    def scores(qi, j, bufs, near):
        s_ref, smax_ref, shift_ref = bufs
        kt = k_ref[pl.ds(pl.multiple_of(j * tile, tile), tile), :]
        for c in range(2):
            qz = qz_ref[c, pl.ds(pl.multiple_of(qi * tile, tile), tile), :]
            s = lax.dot_general(kt, qz, (((1,), (1,)), ((), ())), preferred_element_type=F32)
            if near:
                s = s + bias_ref[c, j - qi + 1]
                shift = jnp.zeros((1, tile), F32)
            else:
                shift = jnp.full((1, tile), jnp.where(j < qi, relb_ref[far_bucket_neg, 2 * h + c],
                                                      relb_ref[far_bucket_pos, 2 * h + c]) * LOG2E, F32)
            s_ref[c] = s
            smax_ref[c] = jnp.max(s, axis=0, keepdims=True) + shift
            shift_ref[c] = shift

    def accumulate(j, bufs):
        s_ref, smax_ref, shift_ref = bufs
        vt = vt_ref[j]
        for c in range(2):
            m_old = m_ref[c]
            m_new = jnp.maximum(m_old, smax_ref[c])
            alpha = jnp.exp2(m_old - m_new)
            e = jnp.exp2(s_ref[c] - (m_new - shift_ref[c])).astype(BF16)
            acc_ref[c] = alpha * acc_ref[c] + jnp.dot(vt, e, preferred_element_type=F32)
            m_ref[c] = m_new

    def finalize(qi):
        outs = []
        for c in range(2):
            acc = acc_ref[c]
            outs.append(acc[:ATT_V_DIM] / acc[ATT_V_DIM:ATT_V_DIM + 1])
        o_t = outs[0] - lam * outs[1]
        y_t = o_t * lax.rsqrt(jnp.mean(o_t * o_t, axis=0, keepdims=True) + EPS)
        o_ref[pl.ds(pl.multiple_of(qi * tile, tile), tile), :] = (
            y_t.T * hg_ref[...] * (1.0 - LAMBDA_INIT)).astype(BF16)
        m_ref[...] = jnp.full(m_ref.shape, -jnp.inf, F32)
        acc_ref[...] = jnp.zeros(acc_ref.shape, F32)

    buf_a = (sa_ref, ma_ref, ha_ref)
    buf_b = (sb_ref, mb_ref, hb_ref)

    def is_near(qi, j):
        return jnp.abs(j - qi) <= 1

    def pair(q1, j1, q2, j2, jc1, jc2, tail=None):
        n1, n2 = is_near(q1, j1), is_near(q2, j2)
        for v1 in (True, False):
            for v2 in (True, False):
                @pl.when(jnp.logical_and(n1 == v1, n2 == v2))
                def _(v1=v1, v2=v2):
                    scores(q1, j1, buf_b, v1)
                    accumulate(jc1, buf_a)
                    scores(q2, j2, buf_a, v2)
                    accumulate(jc2, buf_b)
                    if tail is not None:
                        tail()

    scores(0, 0, buf_a, True)

    def q_body(qi, carry):
        def pair_body(t, c2):
            j = 2 * t
            pair(qi, j + 1, qi, j + 2, j, j + 1)
            return c2

        lax.fori_loop(0, n_tiles // 2 - 1, pair_body, 0)
        pair(qi, n_tiles - 1, jnp.minimum(qi + 1, n_tiles - 1), 0, n_tiles - 2, n_tiles - 1,
             tail=lambda: finalize(qi))
        return carry

    lax.fori_loop(0, n_tiles, q_body, 0)


def _attention(proj, btiles, rel_bias, lam_vecs, head_g, *, tile):
    nb, seq, _ = proj.shape
    assert seq % (2 * tile) == 0 and tile % LANES == 0
    btiles = btiles.reshape(ATT_HEADS, 2, 3, LANES, LANES)
    return pl.pallas_call(
        functools.partial(_attn_body, tile=tile),
        out_shape=jax.ShapeDtypeStruct((nb, seq, ATT_WIDTH), BF16),
        grid=(ATT_HEADS, nb),
        in_specs=[
            pl.BlockSpec(memory_space=pltpu.SMEM),
            pl.BlockSpec((4, ATT_QK_DIM), lambda h, b: (0, 0)),
            pl.BlockSpec((None, seq, LANES), lambda h, b: (b, 0, h)),
            pl.BlockSpec((None, seq, LANES), lambda h, b: (b, 0, ATT_HEADS + h)),
            pl.BlockSpec((None, seq, LANES), lambda h, b: (b, 0, 2 * ATT_HEADS + h)),
            pl.BlockSpec((None, 2, 3, LANES, LANES), lambda h, b: (h, 0, 0, 0, 0)),
            pl.BlockSpec((1, ATT_V_DIM), lambda h, b: (0, h)),
        ],
        out_specs=pl.BlockSpec((None, seq, ATT_V_DIM), lambda h, b: (b, 0, h)),
        scratch_shapes=[
            pltpu.VMEM((2, seq, LANES), BF16),
            pltpu.VMEM((seq // tile, V_ROWS, tile), BF16),
            pltpu.VMEM((2, 3, tile, tile), F32),
            pltpu.VMEM((2, tile, tile), F32),
            pltpu.VMEM((2, tile, tile), F32),
            pltpu.VMEM((2, 1, tile), F32),
            pltpu.VMEM((2, 1, tile), F32),
            pltpu.VMEM((2, 1, tile), F32),
            pltpu.VMEM((2, 1, tile), F32),
            pltpu.VMEM((2, V_ROWS, tile), F32),
            pltpu.VMEM((2, 1, tile), F32),
        ],
        compiler_params=_params(2),
        name="diff_attention",
    )(rel_bias, lam_vecs, proj, proj, proj, btiles, head_g)


def _log_sigmoid(x):
    return jnp.minimum(x, 0.0) - jnp.log1p(jnp.exp(-jnp.abs(x)))


def _lane_scan(x, reverse):
    lane = lax.broadcasted_iota(jnp.int32, x.shape, 1)
    sh = 1
    while sh < LANES:
        if reverse:
            x = x + jnp.where(lane < LANES - sh, pltpu.roll(x, LANES - sh, 1), 0.0)
        else:
            x = x + jnp.where(lane >= sh, pltpu.roll(x, sh, 1), 0.0)
        sh *= 2
    return x


def _mlstm_body(mq_ref, mk_ref, mv_ref, mo_ref, cwq_ref, cwk_ref, cbq_ref, cbk_ref, gates_ref, hg_ref,
                o_ref, q_s, kt_s, gsc, hbuf, c_s):
    seq, dh = mq_ref.shape
    L = MLSTM_CHUNK
    nc = seq // L
    halo = BF16_ROWS

    def conv_chunk(src_ref, w_ref, b_ref, c):
        start = pl.multiple_of(c * L, L)
        cur = src_ref[pl.ds(start, L), :].astype(F32)
        prev_start = pl.multiple_of(jnp.maximum(start - halo, 0), halo)
        next_start = pl.multiple_of(jnp.minimum(start + L, seq - halo), halo)
        prev = src_ref[pl.ds(prev_start, halo), :].astype(F32) * jnp.where(c > 0, 1.0, 0.0)
        nxt = src_ref[pl.ds(next_start, halo), :].astype(F32) * jnp.where(c < nc - 1, 1.0, 0.0)
        ext = jnp.concatenate([prev, cur, nxt], axis=0)
        w = w_ref[...]
        out = jnp.broadcast_to(b_ref[...], (L, dh))
        pad = CONV_WIDTH // 2
        for t in range(CONV_WIDTH):
            lo = halo + t - pad
            out = out + ext[lo:lo + L, :] * w[t:t + 1, :]
        return out * jax.nn.sigmoid(out)

    def prep(c, carry):
        start = pl.multiple_of(c * L, L)
        qc = conv_chunk(mq_ref, cwq_ref, cbq_ref, c) * (MLSTM_HEAD_DIM ** -0.5)
        q_s[pl.ds(start, L), :] = qc.astype(BF16)
        kc = conv_chunk(mk_ref, cwk_ref, cbk_ref, c)
        kt_s[c] = kc.T.astype(BF16)
        return carry

    lax.fori_loop(0, nc, prep, 0)

    g = gates_ref[...]
    b_f = _lane_scan(_log_sigmoid(g[2]), reverse=False)
    b_b = _lane_scan(_log_sigmoid(g[3]), reverse=True)
    gsc[0] = b_f
    gsc[1] = g[0] - b_f
    gsc[2] = b_b
    gsc[3] = g[1] - b_b

    c_s[...] = jnp.zeros(c_s.shape, F32)

    row_i = lax.broadcasted_iota(jnp.int32, (L, L), 0)
    col_i = lax.broadcasted_iota(jnp.int32, (L, L), 1)
    lane_row = lax.broadcasted_iota(jnp.int32, (1, L), 1)
    ones_col = jnp.where(lax.broadcasted_iota(jnp.int32, (L, LANES), 1) == 0, 1.0, 0.0).astype(BF16)

    def chunk_of(k, direction):
        return k if direction == 0 else nc - 1 - k

    def chunk_step(k, m, direction):
        c = chunk_of(k, direction)
        start = pl.multiple_of(c * L, L)
        causal = (col_i <= row_i) if direction == 0 else (col_i >= row_i)
        last_lane = L - 1 if direction == 0 else 0
        q = q_s[pl.ds(start, L), :]
        kt = kt_s[c]
        vaug = jnp.concatenate([mv_ref[pl.ds(start, L), :], ones_col], axis=1)
        brow = gsc[2 * direction, pl.ds(c, 1), :]
        rrow = gsc[2 * direction + 1, pl.ds(c, 1), :]
        bcol = jnp.sum(jnp.where(row_i == col_i, brow, 0.0), axis=1, keepdims=True)
        dmat = jnp.where(causal, bcol + rrow, -jnp.inf)
        dmax = jnp.max(dmat, axis=1, keepdims=True)
        c_old = c_s[direction]
        q_out = jnp.dot(q, jnp.concatenate([kt, c_old.astype(BF16)], axis=1),
                        preferred_element_type=F32)
        qk, p_inter = q_out[:, :L], q_out[:, L:]
        st = (qk * jnp.exp(dmat - dmax)).astype(BF16)
        b_last = jnp.sum(jnp.where(lane_row == last_lane, brow, 0.0), axis=1, keepdims=True)
        grow = b_last + rrow
        gmax = jnp.max(grow, axis=1, keepdims=True)
        wk = jnp.exp(grow - gmax)
        ktw = (kt.astype(F32) * wk).astype(BF16)
        v_out = jnp.dot(jnp.concatenate([st, ktw], axis=0), vaug, preferred_element_type=F32)
        p_intra, upd = v_out[:L], v_out[L:]
        inter = bcol + m
        m_t = jnp.maximum(inter, dmax)
        num_aug = jnp.exp(inter - m_t) * p_inter + jnp.exp(dmax - m_t) * p_intra
        den = num_aug[:, dh:dh + 1]
        hout = num_aug[:, :dh] / jnp.maximum(jnp.abs(den), jnp.exp(-m_t))
        m_new = jnp.maximum(b_last + m, gmax)
        c_s[direction] = jnp.exp(b_last + m - m_new) * c_old + jnp.exp(gmax - m_new) * upd
        return hout, m_new

    def finalize(c, hm):
        start = pl.multiple_of(c * L, L)
        y = _rms(hm, hg_ref[...])
        og = jax.nn.sigmoid(mo_ref[pl.ds(start, L), :].astype(F32))
        o_ref[pl.ds(start, L), :] = (og * y).astype(BF16)

    def run_half(first_step, ms, second_half):
        def body(k, ms):
            out = []
            for direction in range(2):
                hout, m_new = chunk_step(k, ms[direction], direction)
                rows = pl.ds(pl.multiple_of(chunk_of(k, direction) * L, L), L)
                if second_half:
                    finalize(chunk_of(k, direction), hout + hbuf[rows, :])
                else:
                    hbuf[rows, :] = hout
                out.append(m_new)
            return tuple(out)

        return lax.fori_loop(first_step, first_step + nc // 2, body, ms, unroll=2)

    m0 = jnp.zeros((1, 1), F32)
    ms = run_half(0, (m0, m0), False)
    run_half(nc // 2, ms, True)


def _mlstm(proj, gates, conv_w, conv_b, head_g):
    nb, seq, _ = proj.shape
    dh = MLSTM_HEAD_DIM
    nc = seq // MLSTM_CHUNK
    assert seq % MLSTM_CHUNK == 0 and nc % 2 == 0
    base = 3 * ATT_WIDTH // dh
    nh = MLSTM_HEADS

    def col(group):
        return pl.BlockSpec((None, seq, dh), lambda b, h: (b, 0, base + group * nh + h))

    return pl.pallas_call(
        _mlstm_body,
        out_shape=jax.ShapeDtypeStruct((nb, seq, MLSTM_WIDTH), BF16),
        grid=(nb, nh),
        in_specs=[
            col(0), col(1), col(2), col(3),
            pl.BlockSpec((CONV_WIDTH, dh), lambda b, h: (0, h)),
            pl.BlockSpec((CONV_WIDTH, dh), lambda b, h: (0, nh + h)),
            pl.BlockSpec((1, dh), lambda b, h: (0, h)),
            pl.BlockSpec((1, dh), lambda b, h: (0, nh + h)),
            pl.BlockSpec((4, None, None, nc, MLSTM_CHUNK), lambda b, h: (0, h, b, 0, 0)),
            pl.BlockSpec((1, dh), lambda b, h: (0, h)),
        ],
        out_specs=pl.BlockSpec((None, seq, dh), lambda b, h: (b, 0, h)),
        scratch_shapes=[
            pltpu.VMEM((seq, dh), BF16),
            pltpu.VMEM((nc, dh, MLSTM_CHUNK), BF16),
            pltpu.VMEM((4, nc, MLSTM_CHUNK), F32),
            pltpu.VMEM((seq, dh), F32),
            pltpu.VMEM((2, dh, dh + LANES), F32),
        ],
        compiler_params=_params(2),
        name="mlstm",
    )(proj, proj, proj, proj, conv_w, conv_w, conv_b, conv_b, gates, head_g)


def _outproj_body(att_ref, ml_ref, wa_ref, wm_ref, x_ref, g_ref, o_ref):
    mixed = (jnp.dot(att_ref[...], wa_ref[...], preferred_element_type=F32)
             + jnp.dot(ml_ref[...], wm_ref[...], preferred_element_type=F32))
    o_ref[...] = x_ref[...] + _rms(mixed, g_ref[...])


def _out_proj(att, ml, w_out, x, g, *, tm):
    t, d = x.shape
    wa = att.shape[1]
    wm = ml.shape[1]
    assert wa == wm
    return pl.pallas_call(
        _outproj_body,
        out_shape=jax.ShapeDtypeStruct((t, d), F32),
        grid=(t // tm,),
        in_specs=[
            pl.BlockSpec((tm, wa), lambda i: (i, 0)),
            pl.BlockSpec((tm, wm), lambda i: (i, 0)),
            pl.BlockSpec((wa, d), lambda i: (0, 0)),
            pl.BlockSpec((wm, d), lambda i: (1, 0)),
            pl.BlockSpec((tm, d), lambda i: (i, 0)),
            pl.BlockSpec((1, d), lambda i: (0, 0)),
        ],
        out_specs=pl.BlockSpec((tm, d), lambda i: (i, 0)),
        compiler_params=_params(1),
        name="out_proj",
    )(att, ml, w_out, w_out, x, g)


def _layer(xs, rel_bias, ffn1_pre_g, ffn1_post_g, ffn1_w_gate, ffn1_w_up, ffn1_w_down,
           mix_pre_g, mix_post_g, w_in, gate_bias, conv_w, conv_b, lam_vecs,
           att_head_g, mlstm_head_g, w_out,
           ffn2_pre_g, ffn2_post_g, ffn2_w_gate, ffn2_w_up, ffn2_w_down,
           *, tm, tm_proj, tf, tn, attn_tile):
    seq, d = xs[0].shape[1:]
    rows = tuple(x.shape[0] * seq for x in xs)
    t = sum(rows)
    nb = t // seq
    row = lambda v: v.reshape(1, -1).astype(F32)

    x1, w2_gate, w2_up, w2_down = _ffn(
        tuple(x.reshape(-1, d) for x in xs), row(ffn1_pre_g), row(ffn1_post_g),
        ffn1_w_gate.astype(BF16), ffn1_w_up.astype(BF16), ffn1_w_down.astype(BF16), (t,),
        tm=tm, tf=tf, casts=(ffn2_w_gate, ffn2_w_up, ffn2_w_down))

    w_main = w_in.astype(BF16)
    w_gate = jnp.pad(w_main[:, MAIN_COLS:], ((0, 0), (0, LANES - N_GATES)))
    gbias = jnp.pad(row(gate_bias), ((0, 0), (0, LANES - N_GATES)))
    col_scale = jnp.concatenate([jnp.full((1, ATT_WIDTH), ATT_QK_DIM ** -0.5 * LOG2E, F32),
                                 jnp.ones((1, MAIN_COLS - ATT_WIDTH), F32)], axis=1)
    proj, gates = _in_proj(x1, row(mix_pre_g), w_main, col_scale, w_gate, gbias, tm=tm_proj, tn=tn)
    proj = proj.reshape(nb, seq, MAIN_COLS)
    nc = seq // MLSTM_CHUNK
    gates = gates[:N_GATES].reshape(N_GATES // MLSTM_HEADS, MLSTM_HEADS, nb, nc, MLSTM_CHUNK)

    btiles = _bias_tiles(rel_bias.astype(F32))
    att = _attention(proj, btiles, rel_bias.astype(F32), lam_vecs, row(att_head_g), tile=attn_tile)
    ml = _mlstm(proj, gates, conv_w.astype(F32), row(conv_b), row(mlstm_head_g))

    x2 = _out_proj(att.reshape(t, ATT_WIDTH), ml.reshape(t, MLSTM_WIDTH), w_out.astype(BF16),
                   x1, row(mix_post_g), tm=tm)
    ys = _ffn((x2,), row(ffn2_pre_g), row(ffn2_post_g), w2_gate, w2_up, w2_down, rows, tm=tm, tf=tf)
    return tuple(y.reshape(x.shape) for y, x in zip(ys, xs))


def kernel(x_prompt, x_sample, rel_bias, ffn1_pre_g, ffn1_post_g, ffn1_w_gate, ffn1_w_up, ffn1_w_down, mix_pre_g, mix_post_g, w_in, gate_bias, conv_w, conv_b, lambda_q1, lambda_k1, lambda_q2, lambda_k2, att_head_g, mlstm_head_g, w_out, ffn2_pre_g, ffn2_post_g, ffn2_w_gate, ffn2_w_up, ffn2_w_down, *, tm=512, tm_proj=1024, tf=512, tn=1792, attn_tile=512):
    assert x_prompt.shape[1:] == x_sample.shape[1:]
    assert ffn1_pre_g.shape[0] == 1, "single-layer trunk"
    lam_vecs = jnp.concatenate([lambda_q1, lambda_k1, lambda_q2, lambda_k2], axis=0).astype(F32)
    return _layer((x_prompt, x_sample), rel_bias, ffn1_pre_g[0], ffn1_post_g[0], ffn1_w_gate[0], ffn1_w_up[0], ffn1_w_down[0],
               mix_pre_g[0], mix_post_g[0], w_in[0], gate_bias[0], conv_w[0], conv_b[0], lam_vecs,
               att_head_g[0], mlstm_head_g[0], w_out[0],
               ffn2_pre_g[0], ffn2_post_g[0], ffn2_w_gate[0], ffn2_w_up[0], ffn2_w_down[0],
               tm=tm, tm_proj=tm_proj, tf=tf, tn=tn, attn_tile=attn_tile)
```
